```python
import jax, jax.numpy as jnp
from jax import lax
import numpy as np

D_MODEL = 1024
BATCH = 2
SEQ = 8192
DEPTH = 4

GRID_W = 64
CTX_LEN = 256
N_BRANCH = 3
BRANCH_W = 512
POOL_WINDOWS = (2, 4, 8, 16)
POOL_GROUPS = 4
POOL_GW = BRANCH_W // POOL_GROUPS
RWKV_HEADS = 8
RWKV_HD = BRANCH_W // RWKV_HEADS
DECAY_LORA = 32
ICLR_LORA = 32
GATE_LORA = 96
GN_EPS = 64e-5
NAT_HEADS = 8
NAT_HD = BRANCH_W // NAT_HEADS
NAT_KH_MAX = 8
NAT_KW = 16
NAT_SCALE = NAT_HD ** -0.5
D_FF = 2816
N_EXPERTS = 8
TOP_K = 2
N_DENSE = (DEPTH + 1) // 2
N_MOE = DEPTH // 2
RMS_EPS = 1e-6
IN_RWKV = 3 * BRANCH_W + 2 * DECAY_LORA + 2 * ICLR_LORA + GATE_LORA
OFF_RWKV = BRANCH_W
OFF_Q = OFF_RWKV + IN_RWKV
OFF_K = OFF_Q + BRANCH_W
OFF_V = OFF_K + BRANCH_W
OFF_GATE = OFF_V + BRANCH_W
IN_TOTAL = OFF_GATE + N_BRANCH * D_MODEL

kernel_name = 'hybrid_pool_rwkv7_nat_moe_dit'


def rms_norm(x, g):
    xf = x.astype(jnp.float32)
    y = xf * lax.rsqrt(jnp.mean(xf * xf, axis=-1, keepdims=True) + RMS_EPS)
    return (y * g.astype(jnp.float32)).astype(x.dtype)


def split_heads(t, n_heads):
    return t.reshape(*t.shape[:-1], n_heads, t.shape[-1] // n_heads)


def centred_pool_minus_identity(z):
    B, T, _ = z.shape
    zf = z.astype(jnp.float32)
    csum = jnp.concatenate([jnp.zeros((B, 1, BRANCH_W), jnp.float32), jnp.cumsum(zf, axis=1)], axis=1)
    t = jnp.arange(T)
    groups = []
    for gi, win in enumerate(POOL_WINDOWS):
        lo = jnp.clip(t - win // 2, 0, T)
        hi = jnp.clip(t - win // 2 + win, 0, T)
        sl = slice(gi * POOL_GW, (gi + 1) * POOL_GW)
        cg = csum[:, :, sl]
        mean = (cg[:, hi] - cg[:, lo]) / (hi - lo).astype(jnp.float32)[None, :, None]
        groups.append(mean - zf[:, :, sl])
    return jnp.stack(groups, axis=2)


def pool_branch(z, pool_w, pool_scale):
    B, T, _ = z.shape
    p = centred_pool_minus_identity(z)
    return jnp.einsum('btgc,gcd->btgd', p, pool_w).reshape(B, T, BRANCH_W) * pool_scale


def centred_shift(z, mu_prev, mu_next):
    zp = jnp.pad(z, ((0, 0), (1, 0), (0, 0)))[:, :-1]
    zn = jnp.pad(z, ((0, 0), (0, 1), (0, 0)))[:, 1:]
    return z + mu_prev * (zp - z) + mu_next * (zn - z)


def rwkv_prepare(zr, w0, w2, a0, a2, k_kk, k_ka):
    B, T, _ = zr.shape
    f32 = jnp.float32
    r, k, v, wlo, alo, glo = jnp.split(zr, [BRANCH_W, 2 * BRANCH_W, 3 * BRANCH_W, 3 * BRANCH_W + 2 * DECAY_LORA, 3 * BRANCH_W + 2 * DECAY_LORA + 2 * ICLR_LORA], axis=-1)
    wlo = wlo.reshape(B, T, 2, DECAY_LORA)
    alo = alo.reshape(B, T, 2, ICLR_LORA)
    w_log = -jax.nn.softplus(-(w0 + jnp.einsum('btdr,drc->btdc', jnp.tanh(wlo), w2)).astype(f32)) - 0.5
    decay = jnp.exp(-jnp.exp(w_log))
    a = jax.nn.sigmoid((a0 + jnp.einsum('btdr,drc->btdc', alo, a2)).astype(f32))
    kk = split_heads((k * k_kk).astype(f32), RWKV_HEADS)
    kk = kk / jnp.maximum(jnp.sqrt(jnp.sum(kk * kk, axis=-1, keepdims=True)), 1e-12)
    k_dir = k.astype(f32)[:, :, None] * (1 + (a - 1) * k_ka.astype(f32))
    hd = lambda t: t.reshape(B, T, 2, RWKV_HEADS, RWKV_HD)
    kka = hd(a) * kk[:, :, None]
    return (split_heads(r.astype(f32), RWKV_HEADS), split_heads(v.astype(f32), RWKV_HEADS), kk, hd(decay), hd(k_dir), kka, glo)


def wkv_scan(state0, decay, k, v, kk, kka, r, reverse):
    seq = lambda t: jnp.moveaxis(t, 1, 0)
    xs = (seq(decay), seq(k), seq(v), seq(kk), seq(kka))
    if r is not None:
        xs = xs + (seq(r),)

    def step(S, inp):
        w_t, k_t, v_t, kk_t, kka_t = inp[:5]
        sa = jnp.einsum('bhij,bhj->bhi', S, kk_t)
        S = S * w_t[:, :, None, :] - sa[..., None] * kka_t[:, :, None, :] + v_t[..., None] * k_t[:, :, None, :]
        out = jnp.einsum('bhij,bhj->bhi', S, inp[5]) if len(inp) == 6 else None
        return S, out

    S, outs = lax.scan(step, state0, xs, reverse=reverse)
    return S, (None if r is None else jnp.moveaxis(outs, 0, 1))


def bidirectional_wkv(lat, cx, need_ctx_out):
    r, v, kk, decay, k_dir, kka, _ = lat
    rc, vc, kkc, decayc, k_dirc, kkac, _ = cx
    B = r.shape[0]
    outs, outs_c = [], []
    for d, reverse in enumerate((False, True)):
        S0 = jnp.zeros((B, RWKV_HEADS, RWKV_HD, RWKV_HD), jnp.float32)
        S_ctx, o_c = wkv_scan(S0, decayc[:, :, d], k_dirc[:, :, d], vc, kkc, kkac[:, :, d], rc if need_ctx_out else None, reverse)
        _, o = wkv_scan(S_ctx, decay[:, :, d], k_dir[:, :, d], v, kk, kka[:, :, d], r, reverse)
        outs.append(o)
        outs_c.append(o_c)
    return outs[0] + outs[1], (outs_c[0] + outs_c[1] if need_ctx_out else None)


def rwkv_readout(wkv, prep, bonus_rk, gn_g, gn_b, gate_g2):
    r, v, _, _, k_dir, _, glo = prep
    B, T = wkv.shape[:2]
    mu = jnp.mean(wkv, axis=-1, keepdims=True)
    var = jnp.mean(jnp.square(wkv - mu), axis=-1, keepdims=True)
    y = ((wkv - mu) * lax.rsqrt(var + GN_EPS)).reshape(B, T, BRANCH_W) * gn_g + gn_b
    bonus = jnp.sum(r[:, :, None] * k_dir * bonus_rk, axis=(2, 4))[..., None] * v
    gate = jax.nn.sigmoid(glo) @ gate_g2
    return (y + bonus.reshape(B, T, BRANCH_W)) * gate


def neighbourhood_attention(q, k, v, kc, vc, rpb):
    B, T, H, hd = q.shape
    rows = T // GRID_W
    kh = min(NAT_KH_MAX, rows)
    qg = jnp.moveaxis(q.reshape(B, rows, GRID_W, H, hd), 1, 0)
    kg = k.reshape(B, rows, GRID_W, H, hd)
    vg = v.reshape(B, rows, GRID_W, H, hd)
    cols = jnp.arange(GRID_W)
    cstart = jnp.clip(cols - NAT_KW // 2, 0, GRID_W - NAT_KW)
    col_idx = cstart[:, None] + jnp.arange(NAT_KW)
    col_off = col_idx - cols[:, None] + (NAT_KW - 1)
    n_loc = kh * NAT_KW

    def row_fn(args):
        r, q_r = args
        rs = jnp.clip(r - kh // 2, 0, rows - kh)
        k_win = lax.dynamic_slice_in_dim(kg, rs, kh, axis=1)[:, :, col_idx]
        v_win = lax.dynamic_slice_in_dim(vg, rs, kh, axis=1)[:, :, col_idx]
        row_off = rs + jnp.arange(kh) - r + (NAT_KH_MAX - 1)
        bias = rpb[:, row_off[None, :, None], col_off[:, None, :]]
        s_loc = jnp.einsum('bqhd,brqchd->bhqrc', q_r, k_win).astype(jnp.float32) + bias.astype(jnp.float32)
        s_ctx = jnp.einsum('bqhd,bchd->bhqc', q_r, kc).astype(jnp.float32)
        s = jnp.concatenate([s_loc.reshape(B, H, GRID_W, n_loc), s_ctx], axis=-1)
        p = jax.nn.softmax(s, axis=-1).astype(v.dtype)
        p_loc = p[..., :n_loc].reshape(B, H, GRID_W, kh, NAT_KW)
        return jnp.einsum('bhqrc,brqchd->bqhd', p_loc, v_win) + jnp.einsum('bhqc,bchd->bqhd', p[..., n_loc:], vc)

    o = lax.map(row_fn, (jnp.arange(rows), qg))
    return jnp.moveaxis(o, 0, 1).reshape(B, T, H * hd)


def context_attention(qc, kc, vc):
    B, L, H, hd = qc.shape
    s = jnp.einsum('bqhd,bkhd->bhqk', qc, kc).astype(jnp.float32)
    p = jax.nn.softmax(s, axis=-1).astype(vc.dtype)
    return jnp.einsum('bhqk,bkhd->bqhd', p, vc).reshape(B, L, H * hd)


def gated_merge(branches, gate_pre, w_branch, w_out):
    B, T, _ = gate_pre.shape
    proj = jnp.einsum('btnc,ncd->btnd', jnp.stack(branches, axis=2), w_branch)
    gates = jax.nn.sigmoid(gate_pre.reshape(B, T, N_BRANCH, D_MODEL))
    return jnp.sum(gates * proj, axis=2) @ w_out


def token_mixer(h, hc, w_in, pool_w, pool_scale, shift_mu, decay_w0, decay_w2, iclr_a0, iclr_a2, key_kk, key_ka, bonus_rk, gn_g, gn_b, gate_g2, nat_qn_g, nat_kn_g, nat_rpb, w_branch, w_out, need_ctx_out):
    splits = [OFF_RWKV, OFF_Q, OFF_K, OFF_V, OFF_GATE]
    zp, zr, zq, zk, zv, zg = jnp.split(h @ w_in, splits, axis=-1)
    if need_ctx_out:
        zpc, zrc, zqc, zkc, zvc, zgc = jnp.split(hc @ w_in, splits, axis=-1)
    else:
        zrc = hc @ w_in[:, OFF_RWKV:OFF_Q]
        zkc, zvc = jnp.split(hc @ w_in[:, OFF_K:OFF_GATE], 2, axis=-1)
    rwkv_p = (decay_w0, decay_w2, iclr_a0, iclr_a2, key_kk, key_ka)
    lat = rwkv_prepare(centred_shift(zr, shift_mu[0], shift_mu[1]), *rwkv_p)
    cx = rwkv_prepare(centred_shift(zrc, shift_mu[0], shift_mu[1]), *rwkv_p)
    wkv, wkv_c = bidirectional_wkv(lat, cx, need_ctx_out)
    kc = rms_norm(split_heads(zkc, NAT_HEADS), nat_kn_g)
    vc = split_heads(zvc, NAT_HEADS)
    q = rms_norm(split_heads(zq, NAT_HEADS), nat_qn_g) * NAT_SCALE
    k = rms_norm(split_heads(zk, NAT_HEADS), nat_kn_g)
    nat = neighbourhood_attention(q, k, split_heads(zv, NAT_HEADS), kc, vc, nat_rpb)
    branches = (pool_branch(zp, pool_w, pool_scale), rwkv_readout(wkv, lat, bonus_rk, gn_g, gn_b, gate_g2), nat)
    y = gated_merge(branches, zg, w_branch, w_out)
    if not need_ctx_out:
        return y, None
    qc = rms_norm(split_heads(zqc, NAT_HEADS), nat_qn_g) * NAT_SCALE
    branches_c = (pool_branch(zpc, pool_w, pool_scale), rwkv_readout(wkv_c, cx, bonus_rk, gn_g, gn_b, gate_g2), context_attention(qc, kc, vc))
    yc = gated_merge(branches_c, zgc, w_branch, w_out)
    return y, yc


def swiglu(h, w1, w3, w2):
    return (jax.nn.silu(h @ w1) * (h @ w3)) @ w2


def moe_swiglu(h, router, w1, w3, w2):
    logits = (h @ router).astype(jnp.float32)
    top_logit, top_idx = lax.top_k(logits, TOP_K)
    top_w = jax.nn.softmax(top_logit, axis=-1)
    gate = jnp.sum(jax.nn.one_hot(top_idx, N_EXPERTS, dtype=jnp.float32) * top_w[..., None], axis=-2)
    out = jnp.zeros_like(h)
    for e in range(N_EXPERTS):
        out = out + gate[..., e:e + 1].astype(h.dtype) * swiglu(h, w1[e], w3[e], w2[e])
    return out


def channel_mixer(h, li, ffn_w1, ffn_w3, ffn_w2, router, moe_w1, moe_w3, moe_w2):
    j = li // 2
    if li % 2 == 0:
        return swiglu(h, ffn_w1[j], ffn_w3[j], ffn_w2[j])
    return moe_swiglu(h, router[j], moe_w1[j], moe_w3[j], moe_w2[j])


def setup_inputs(seed: int = 0) -> dict:
    key = jax.random.key(seed)
    ks = iter(jax.random.split(key, 40))
    nrm = lambda shape, scale: jax.random.normal(next(ks), shape, jnp.float32) * scale
    unif = lambda shape, lo, hi: jax.random.uniform(next(ks), shape, jnp.float32, lo, hi)
    D = D_MODEL
    return {
        'x': nrm((BATCH, SEQ, D), 1.0),
        'c': nrm((BATCH, D), 1.0),
        'ctx': nrm((BATCH, CTX_LEN, D), 1.0),
        'c_ctx': nrm((D,), 1.0),
        'w_mod': nrm((DEPTH, D, 6 * D), 0.5 * D ** -0.5),
        'b_mod': nrm((DEPTH, 6 * D), 0.02),
        'norm_mix_g': 1.0 + nrm((DEPTH, D), 0.05),
        'norm_ffn_g': 1.0 + nrm((DEPTH, D), 0.05),
        'w_in': nrm((DEPTH, D, IN_TOTAL), D ** -0.5),
        'pool_w': nrm((DEPTH, POOL_GROUPS, POOL_GW, POOL_GW), POOL_GW ** -0.5),
        'pool_scale': 1.0 + nrm((DEPTH, BRANCH_W), 0.1),
        'shift_mu': unif((DEPTH, 2, IN_RWKV), 0.0, 0.5),
        'decay_w0': unif((DEPTH, 2, BRANCH_W), -6.0, 1.0),
        'decay_w2': nrm((DEPTH, 2, DECAY_LORA, BRANCH_W), 0.5 * DECAY_LORA ** -0.5),
        'iclr_a0': nrm((DEPTH, 2, BRANCH_W), 0.5),
        'iclr_a2': nrm((DEPTH, 2, ICLR_LORA, BRANCH_W), 0.5 * ICLR_LORA ** -0.5),
        'key_kk': 0.85 + nrm((DEPTH, BRANCH_W), 0.05),
        'key_ka': 1.0 + nrm((DEPTH, BRANCH_W), 0.05),
        'bonus_rk': nrm((DEPTH, RWKV_HEADS, RWKV_HD), 0.1),
        'gn_g': 1.0 + nrm((DEPTH, BRANCH_W), 0.05),
        'gn_b': nrm((DEPTH, BRANCH_W), 0.02),
        'gate_g2': nrm((DEPTH, GATE_LORA, BRANCH_W), GATE_LORA ** -0.5),
        'nat_qn_g': 1.0 + nrm((DEPTH, NAT_HD), 0.05),
        'nat_kn_g': 1.0 + nrm((DEPTH, NAT_HD), 0.05),
        'nat_rpb': nrm((DEPTH, NAT_HEADS, 2 * NAT_KH_MAX - 1, 2 * NAT_KW - 1), 0.2),
        'w_branch': nrm((DEPTH, N_BRANCH, BRANCH_W, D), BRANCH_W ** -0.5),
        'w_out': nrm((DEPTH, D, D), D ** -0.5),
        'ffn_w1': nrm((N_DENSE, D, D_FF), D ** -0.5),
        'ffn_w3': nrm((N_DENSE, D, D_FF), D ** -0.5),
        'ffn_w2': nrm((N_DENSE, D_FF, D), D_FF ** -0.5),
        'router': nrm((N_MOE, D, N_EXPERTS), D ** -0.5),
        'moe_w1': nrm((N_MOE, N_EXPERTS, D, D_FF), D ** -0.5),
        'moe_w3': nrm((N_MOE, N_EXPERTS, D, D_FF), D ** -0.5),
        'moe_w2': nrm((N_MOE, N_EXPERTS, D_FF, D), D_FF ** -0.5),
    }


def reference(x, c, ctx, c_ctx, w_mod, b_mod, norm_mix_g, norm_ffn_g, w_in, pool_w, pool_scale, shift_mu, decay_w0, decay_w2, iclr_a0, iclr_a2, key_kk, key_ka, bonus_rk, gn_g, gn_b, gate_g2, nat_qn_g, nat_kn_g, nat_rpb, w_branch, w_out, ffn_w1, ffn_w3, ffn_w2, router, moe_w1, moe_w3, moe_w2):
    xc = ctx
    for li in range(DEPTH):
        last = li == DEPTH - 1
        mod = jax.nn.silu(c) @ w_mod[li] + b_mod[li]
        mod_c = jax.nn.silu(c_ctx) @ w_mod[li] + b_mod[li]
        sh1, sc1, g1, sh2, sc2, g2 = jnp.split(mod[:, None, :], 6, axis=-1)
        sh1c, sc1c, g1c, sh2c, sc2c, g2c = jnp.split(mod_c, 6)
        h = rms_norm(x, norm_mix_g[li]) * (1 + sc1) + sh1
        hc = rms_norm(xc, norm_mix_g[li]) * (1 + sc1c) + sh1c
        y, yc = token_mixer(h, hc, w_in[li], pool_w[li], pool_scale[li], shift_mu[li], decay_w0[li], decay_w2[li], iclr_a0[li], iclr_a2[li], key_kk[li], key_ka[li], bonus_rk[li], gn_g[li], gn_b[li], gate_g2[li], nat_qn_g[li], nat_kn_g[li], nat_rpb[li], w_branch[li], w_out[li], not last)
        x = x + g1 * y
        h = rms_norm(x, norm_ffn_g[li]) * (1 + sc2) + sh2
        x = x + g2 * channel_mixer(h, li, ffn_w1, ffn_w3, ffn_w2, router, moe_w1, moe_w3, moe_w2)
        if not last:
            xc = xc + g1c * yc
            hc = rms_norm(xc, norm_ffn_g[li]) * (1 + sc2c) + sh2c
            xc = xc + g2c * channel_mixer(hc, li, ffn_w1, ffn_w3, ffn_w2, router, moe_w1, moe_w3, moe_w2)
    return x
```

```python
import functools

import jax
import jax.numpy as jnp
from jax import lax
from jax.experimental import pallas as pl
from jax.experimental.pallas import tpu as pltpu

F32 = jnp.float32
BF16 = jnp.bfloat16

GRID_W = 64
BRANCH_W = 512
POOL_WINDOWS = (2, 4, 8, 16)
POOL_GW = 128
HEADS = 8
HD = 64
DECAY_LORA = 32
ICLR_LORA = 32
GATE_LORA = 96
GN_EPS = 64e-5
NAT_KH = 8
NAT_KW = 16
NAT_SCALE = HD ** -0.5
TOP_K = 2
RMS_EPS = 1e-6
NEG = -1e30

ZG_W = 3 * 1024
COL_G = 0
COL_RKV = 3072
COL_Q = 4608
COL_K = 5120
COL_V = 5632
COL_P = 6144
COL_LORA = 6656
Z_W = 6912
LORA_W = 256

CHUNK = 64
POOL_HALO = 64
V7X_VMEM_LIMIT = 56 * 1024 * 1024


def _bdot(a, b):
    return jnp.dot(a.astype(BF16), b.astype(BF16), preferred_element_type=F32)


def _bdot_nt(a, b):
    return lax.dot_general(a.astype(BF16), b.astype(BF16), (((1,), (1,)), ((), ())), preferred_element_type=F32)


def _split2(x):
    hi = x.astype(BF16)
    lo = (x - hi.astype(F32)).astype(BF16)
    return hi, lo


def _dot2(x, b):
    hi, lo = _split2(x)
    return jnp.dot(hi, b, preferred_element_type=F32) + jnp.dot(lo, b, preferred_element_type=F32)


def _ldot3(a, x):
    h1 = x.astype(BF16)
    r1 = x - h1.astype(F32)
    h2 = r1.astype(BF16)
    h3 = (r1 - h2.astype(F32)).astype(BF16)
    d = lambda h: jnp.dot(a, h, preferred_element_type=F32)
    return d(h1) + d(h2) + d(h3)


def _silu(x):
    return x * jax.nn.sigmoid(x)


def _params(sem, vmem=None):
    return pltpu.CompilerParams(dimension_semantics=sem, vmem_limit_bytes=vmem)


def _mod_kernel(c_ref, w_ref, b_ref, o_ref):
    o_ref[...] = jnp.dot(_silu(c_ref[...]), w_ref[...], precision=lax.Precision.HIGHEST,
                         preferred_element_type=F32) + b_ref[...]


def _modulation(cc, w_mod, b_mod):
    depth, d, n = w_mod.shape
    tn = 1536
    return pl.pallas_call(
        _mod_kernel,
        grid=(depth, n // tn),
        in_specs=[pl.BlockSpec((8, d), lambda l, j: (0, 0)),
                  pl.BlockSpec((None, d, tn), lambda l, j: (l, 0, j)),
                  pl.BlockSpec((None, 1, tn), lambda l, j: (l, 0, j))],
        out_specs=pl.BlockSpec((None, 8, tn), lambda l, j: (l, 0, j)),
        out_shape=jax.ShapeDtypeStruct((depth, 8, n), F32),
        compiler_params=_params(("arbitrary", "arbitrary"), 40 * 1024 * 1024),
        name="modulation",
    )(cc, w_mod, b_mod.reshape(depth, 1, n))


def _norm_mod(x, g, modc_ref, modb_ref, row0, ctx_len, k_shift, k_scale):
    rows = x.shape[0]
    isctx = (row0 + lax.broadcasted_iota(jnp.int32, (rows, 1), 0)) < ctx_len
    sh = jnp.where(isctx, modc_ref[k_shift:k_shift + 1, :], modb_ref[k_shift:k_shift + 1, :])
    sc = jnp.where(isctx, modc_ref[k_scale:k_scale + 1, :], modb_ref[k_scale:k_scale + 1, :])
    ms = jnp.mean(x * x, axis=-1, keepdims=True)
    return (x * lax.rsqrt(ms + RMS_EPS) * g) * (1.0 + sc) + sh


def _gate_rows(modc_ref, modb_ref, row0, rows, ctx_len, k_gate):
    isctx = (row0 + lax.broadcasted_iota(jnp.int32, (rows, 1), 0)) < ctx_len
    return jnp.where(isctx, modc_ref[k_gate:k_gate + 1, :], modb_ref[k_gate:k_gate + 1, :])


def _inproj_kernel(x_ref, g_ref, modc_ref, modb_ref, w_ref, z_ref, *, tt, ctx_len):
    i = pl.program_id(2)
    h = _norm_mod(x_ref[...], g_ref[...], modc_ref, modb_ref, i * tt, ctx_len, 0, 1)
    z_ref[...] = _bdot(h, w_ref[...])


def _inproj(xa, g, modc, modb, w, tt, ctx_len):
    b, ta, d = xa.shape
    nh = 2
    tn = Z_W // nh
    return pl.pallas_call(
        functools.partial(_inproj_kernel, tt=tt, ctx_len=ctx_len),
        grid=(nh, b, ta // tt),
        in_specs=[pl.BlockSpec((None, tt, d), lambda n, bb, i: (bb, i, 0)),
                  pl.BlockSpec((1, d), lambda n, bb, i: (0, 0)),
                  pl.BlockSpec((6, d), lambda n, bb, i: (0, 0)),
                  pl.BlockSpec((None, 6, d), lambda n, bb, i: (bb, 0, 0)),
                  pl.BlockSpec((d, tn), lambda n, bb, i: (0, n))],
        out_specs=pl.BlockSpec((None, tt, tn), lambda n, bb, i: (bb, i, n)),
        out_shape=jax.ShapeDtypeStruct((b, ta, Z_W), F32),
        compiler_params=_params(("arbitrary",) * 3, 48 * 1024 * 1024),
        name="inproj",
    )(xa, g, modc, modb, w)


def _pool_kernel(zp_ref, zc_ref, zn_ref, pw_ref, ps_ref, o_ref, *, tt, ctx_len, ta):
    i = pl.program_id(1)
    zc = zc_ref[...]
    zcat = jnp.concatenate([zp_ref[...], zc, zn_ref[...]], axis=0)
    t_g = i * tt + lax.broadcasted_iota(jnp.int32, (tt, 1), 0)
    s_g = i * tt - POOL_HALO + lax.broadcasted_iota(jnp.int32, (tt, tt + 2 * POOL_HALO), 1)
    in_ctx = i * tt < ctx_len
    seg_lo = jnp.where(in_ctx, 0, ctx_len)
    seg_hi = jnp.where(in_ctx, ctx_len, ta)
    for gi, win in enumerate(POOL_WINDOWS):
        sl = slice(gi * POOL_GW, (gi + 1) * POOL_GW)
        lo = jnp.clip(t_g - win // 2, seg_lo, seg_hi)
        hi = jnp.clip(t_g - win // 2 + win, seg_lo, seg_hi)
        msk = jnp.where(s_g >= lo, jnp.where(s_g < hi, 1.0, 0.0), 0.0).astype(BF16)
        hi_b, lo_b = _split2(zcat[:, sl])
        wsum = jnp.dot(msk, hi_b, preferred_element_type=F32) + jnp.dot(msk, lo_b, preferred_element_type=F32)
        p = wsum / (hi - lo).astype(F32) - zc[:, sl]
        o_ref[:, sl] = _bdot(p, pw_ref[gi]) * ps_ref[:, sl]


def _pool(z, pool_w, pool_scale, tt, ctx_len):
    b, ta, _ = z.shape
    halo = POOL_HALO
    nb = ta // halo
    cb = COL_P // BRANCH_W
    return pl.pallas_call(
        functools.partial(_pool_kernel, tt=tt, ctx_len=ctx_len, ta=ta),
        grid=(b, ta // tt),
        in_specs=[pl.BlockSpec((None, halo, BRANCH_W), lambda bb, i: (bb, jnp.maximum(i * (tt // halo) - 1, 0), cb)),
                  pl.BlockSpec((None, tt, BRANCH_W), lambda bb, i: (bb, i, cb)),
                  pl.BlockSpec((None, halo, BRANCH_W), lambda bb, i: (bb, jnp.minimum((i + 1) * (tt // halo), nb - 1), cb)),
                  pl.BlockSpec((4, POOL_GW, POOL_GW), lambda bb, i: (0, 0, 0)),
                  pl.BlockSpec((1, BRANCH_W), lambda bb, i: (0, 0))],
        out_specs=pl.BlockSpec((None, tt, BRANCH_W), lambda bb, i: (bb, i, 0)),
        out_shape=jax.ShapeDtypeStruct((b, ta, BRANCH_W), F32),
        compiler_params=_params(("arbitrary",) * 2),
        name="pool",
    )(z, z, z, pool_w, pool_scale)


def _shifted(z, zprev, znext, mu, first, last):
    tt = z.shape[0]
    ridx = lax.broadcasted_iota(jnp.int32, (tt, 1), 0)
    zp = pltpu.roll(z, 1, 0)
    zp = jnp.where(ridx == 0, zprev[7:8, :], zp)
    zp = jnp.where(first, 0.0, zp)
    zn = pltpu.roll(z, tt - 1, 0)
    zn = jnp.where(ridx == tt - 1, znext[0:1, :], zn)
    zn = jnp.where(last, 0.0, zn)
    return z + mu[0:1, :] * (zp - z) + mu[1:2, :] * (zn - z)


def _prep_kernel(rp_ref, rc_ref, rn_ref, lp_ref, lc_ref, ln_ref, mur_ref, mul_ref, w0_ref, w2_ref, a0_ref, a2_ref,
                 g2_ref, kkw_ref, kaw_ref, brk_ref, bd_ref,
                 r_ref, v_ref, kk_ref, lw_ref, kd_ref, ka_ref, bon_ref, gate_ref, *, tt, ctx_len, ta):
    i = pl.program_id(1)
    grow = i * tt + lax.broadcasted_iota(jnp.int32, (tt, 1), 0)
    first = (grow == 0) | (grow == ctx_len)
    last = (grow == ctx_len - 1) | (grow == ta - 1)
    rkv = _shifted(rc_ref[...], rp_ref[...], rn_ref[...], mur_ref[...], first, last)
    lo = _shifted(lc_ref[...], lp_ref[...], ln_ref[...], mul_ref[...], first, last)
    c = BRANCH_W
    r, k, v = rkv[:, :c], rkv[:, c:2 * c], rkv[:, 2 * c:]
    y = _bdot(jnp.tanh(lo), w2_ref[...]) + w0_ref[...]
    lw = -jnp.exp(-0.5) * jax.nn.sigmoid(y)
    a = jax.nn.sigmoid(_bdot(lo, a2_ref[...]) + a0_ref[...])
    gate_ref[...] = _bdot(jax.nn.sigmoid(lo), g2_ref[...])
    bd = bd_ref[...]
    kk = k * kkw_ref[...]
    ss = _dot2(kk * kk, bd)
    kk = kk / jnp.maximum(jnp.sqrt(ss), 1e-12)
    a_f, a_b = a[:, :c], a[:, c:]
    kaw = kaw_ref[...]
    kd_f = k * (1.0 + (a_f - 1.0) * kaw)
    kd_b = k * (1.0 + (a_b - 1.0) * kaw)
    r_ref[...] = r
    v_ref[...] = v
    kk_ref[...] = kk
    lw_ref[...] = lw
    kd_ref[:, :c] = kd_f
    kd_ref[:, c:] = kd_b
    ka_ref[:, :c] = a_f * kk
    ka_ref[:, c:] = a_b * kk
    bon_ref[...] = _dot2(r * (kd_f + kd_b) * brk_ref[...], bd) * v


def _prep(z, mur, mul, w0, w2f, a0, a2f, g2f, kkw, kaw, brk, bd, tt, ctx_len):
    b, ta, _ = z.shape
    nb = ta // 8
    c = BRANCH_W
    rb, lb = COL_RKV // (3 * c), COL_LORA // LORA_W
    prev = lambda bb, i: (bb, jnp.maximum(i * (tt // 8) - 1, 0))
    nxt = lambda bb, i: (bb, jnp.minimum((i + 1) * (tt // 8), nb - 1))
    full = lambda shape: pl.BlockSpec(shape, lambda bb, i: (0,) * len(shape))
    tok = lambda w: pl.BlockSpec((None, tt, w), lambda bb, i: (bb, i, 0))
    sds = lambda w: jax.ShapeDtypeStruct((b, ta, w), F32)
    return pl.pallas_call(
        functools.partial(_prep_kernel, tt=tt, ctx_len=ctx_len, ta=ta),
        grid=(b, ta // tt),
        in_specs=[pl.BlockSpec((None, 8, 3 * c), lambda bb, i: prev(bb, i) + (rb,)),
                  pl.BlockSpec((None, tt, 3 * c), lambda bb, i: (bb, i, rb)),
                  pl.BlockSpec((None, 8, 3 * c), lambda bb, i: nxt(bb, i) + (rb,)),
                  pl.BlockSpec((None, 8, LORA_W), lambda bb, i: prev(bb, i) + (lb,)),
                  pl.BlockSpec((None, tt, LORA_W), lambda bb, i: (bb, i, lb)),
                  pl.BlockSpec((None, 8, LORA_W), lambda bb, i: nxt(bb, i) + (lb,)),
                  full((2, 3 * c)), full((2, LORA_W)), full((1, 2 * c)), full((LORA_W, 2 * c)), full((1, 2 * c)),
                  full((LORA_W, 2 * c)), full((LORA_W, c)), full((1, c)), full((1, c)), full((1, c)), full((c, c))],
        out_specs=[tok(c), tok(c), tok(c), tok(2 * c), tok(2 * c), tok(2 * c), tok(c), tok(c)],
        out_shape=[sds(c), sds(c), sds(c), sds(2 * c), sds(2 * c), sds(2 * c), sds(c), sds(c)],
        compiler_params=_params(("arbitrary",) * 2, 40 * 1024 * 1024),
        name="rwkv_prep",
    )(z, z, z, z, z, z, mur, mul, w0, w2f, a0, a2f, g2f, kkw, kaw, brk, bd)


def _wkv_kernel(r_ref, v_ref, kk_ref, lw_ref, kd_ref, ka_ref, o_ref, st_ref):
    d = pl.program_id(1)
    ci = pl.program_id(2)
    C = CHUNK

    @pl.when(ci == 0)
    def _():
        st_ref[...] = jnp.zeros_like(st_ref)

    rev = d == 1
    sgn = 1 - 2 * d
    row = lax.broadcasted_iota(jnp.int32, (C, C), 0)
    col = lax.broadcasted_iota(jnp.int32, (C, C), 1)
    tri = jnp.where((col - row) * sgn <= 0, 1.0, 0.0).astype(BF16)
    lw = lw_ref[...]
    cs = _ldot3(tri, lw)
    tot = jnp.where(rev, cs[0:1, :], cs[C - 1:C, :])
    p_in = jnp.exp(cs)
    p_ex = jnp.exp(cs - lw)
    p_inv = jnp.exp(-cs)
    p_end = jnp.exp(tot - cs)
    p_tot = jnp.exp(tot)
    kd = kd_ref[...]
    ka = ka_ref[...]
    kq = kk_ref[...] * p_ex
    rq = r_ref[...] * p_in
    kdi = kd * p_inv
    kai = ka * p_inv
    kde = kd * p_end
    kae = ka * p_end
    v = v_ref[...]

    t_i = lax.broadcasted_iota(jnp.int32, (C, 2 * C), 0)
    s_i = lax.broadcasted_iota(jnp.int32, (C, 2 * C), 1) & (C - 1)
    strict = (s_i - t_i) * sgn < 0
    incl = (s_i - t_i) * sgn <= 0
    lane = lax.broadcasted_iota(jnp.int32, (1, 2 * HD), 1)
    m1 = jnp.where(lane < HD, 1.0, 0.0)
    m2 = 1.0 - m1
    r2 = lax.broadcasted_iota(jnp.int32, (2 * HD, 2 * HD), 0)
    c2 = lax.broadcasted_iota(jnp.int32, (2 * HD, 2 * HD), 1)
    bdm = (r2 < HD) == (c2 < HD)
    eye = jnp.where(r2 == c2, 1.0, 0.0)
    stack2 = lambda x: jnp.concatenate([x * m1, x * m2], axis=0)

    for p in range(HEADS // 2):
        sl = slice(2 * HD * p, 2 * HD * (p + 1))
        lhs = jnp.concatenate([kq[:, sl], rq[:, sl]], axis=0)
        rhs = jnp.concatenate([stack2(kdi[:, sl]), stack2(kai[:, sl])], axis=0)
        g = _bdot_nt(lhs, rhs)
        akk = jnp.where(strict, g[:C, :2 * C], 0.0)
        aab = jnp.where(strict, g[:C, 2 * C:], 0.0)
        ark = jnp.where(incl, g[C:, :2 * C], 0.0)
        ara = jnp.where(incl, g[C:, 2 * C:], 0.0)
        lmat = stack2(aab)
        tm = eye - lmat
        pw = _bdot(lmat, lmat)
        n_sq = CHUNK.bit_length() - 2
        for it in range(n_sq):
            tm = tm + _bdot(tm, pw)
            if it + 1 < n_sq:
                pw = _bdot(pw, pw)
        tss = tm[:C, :] + tm[C:, :]
        s0 = st_ref[p]
        uo = _bdot_nt(lhs, s0)
        vp = v[:, sl]
        vst = stack2(vp)
        rhs1 = uo[:C, :] + _bdot(akk, vst)
        u = _bdot(tss, stack2(rhs1))
        o = uo[C:, :] + _bdot(jnp.concatenate([ark, -ara], axis=1), jnp.concatenate([vst, stack2(u)], axis=0))
        o_ref[:, sl] = o
        x = jnp.concatenate([kde[:, sl], -kae[:, sl]], axis=0)
        yv = jnp.concatenate([vp, u], axis=0)
        upd = _bdot(yv.T, x)
        st_ref[p] = s0 * p_tot[:, sl] + jnp.where(bdm, upd, 0.0)


def _wkv(r, v, kk, lw, kd, ka, ctx_len):
    b, ta, c = r.shape
    nch = ta // CHUNK
    ncc = ctx_len // CHUNK

    def chunk(d, ci):
        rev_idx = jnp.where(ci < ncc, ncc - 1 - ci, nch + ncc - 1 - ci)
        return jnp.where(d == 0, ci, rev_idx)

    shared = pl.BlockSpec((None, CHUNK, c), lambda bb, d, ci: (bb, chunk(d, ci), 0))
    per_dir = pl.BlockSpec((None, CHUNK, c), lambda bb, d, ci: (bb, chunk(d, ci), d))
    return pl.pallas_call(
        _wkv_kernel,
        grid=(b, 2, nch),
        in_specs=[shared, shared, shared, per_dir, per_dir, per_dir],
        out_specs=pl.BlockSpec((None, None, CHUNK, c), lambda bb, d, ci: (d, bb, chunk(d, ci), 0)),
        out_shape=jax.ShapeDtypeStruct((2, b, ta, c), F32),
        scratch_shapes=[pltpu.VMEM((HEADS // 2, 2 * HD, 2 * HD), F32)],
        compiler_params=_params(("arbitrary",) * 3),
        name="wkv_scan",
    )(r, v, kk, lw, kd, ka)


def _qknorm_kernel(q_ref, k_ref, v_ref, gq_ref, gk_ref, bd_ref, qo_ref, ko_ref, vo_ref):
    bd = bd_ref[...]

    def norm(x, g):
        ms = _dot2(x * x, bd) * (1.0 / HD)
        return x * lax.rsqrt(ms + RMS_EPS) * g

    qo_ref[...] = (norm(q_ref[...], gq_ref[...]) * NAT_SCALE).astype(BF16)
    ko_ref[...] = norm(k_ref[...], gk_ref[...]).astype(BF16)
    vo_ref[...] = v_ref[...].astype(BF16)


def _qknorm(z, gq, gk, bd, tt):
    b, ta, _ = z.shape
    c = BRANCH_W
    col = lambda cb: pl.BlockSpec((None, tt, c), lambda bb, i: (bb, i, cb))
    full = lambda shape: pl.BlockSpec(shape, lambda bb, i: (0,) * len(shape))
    out = pl.BlockSpec((None, tt, c), lambda bb, i: (bb, i, 0))
    sds = jax.ShapeDtypeStruct((b, ta, c), BF16)
    return pl.pallas_call(
        _qknorm_kernel,
        grid=(b, ta // tt),
        in_specs=[col(COL_Q // c), col(COL_K // c), col(COL_V // c), full((1, c)), full((1, c)), full((c, c))],
        out_specs=[out, out, out],
        out_shape=[sds, sds, sds],
        compiler_params=_params(("arbitrary",) * 2),
        name="nat_qknorm",
    )(z, z, z, gq, gk, bd)


def _nat_kernel(q_ref, kp_ref, kc_ref, kn_ref, vp_ref, vc_ref, vn_ref, kx_ref, vx_ref, tab_ref, o_ref,
                kbuf, vbuf, *, tt, n_rows):
    i = pl.program_id(1)
    rpt = tt // GRID_W
    nwin = NAT_KH * GRID_W
    lane = lax.broadcasted_iota(jnp.int32, (1, 2 * HD), 1)
    first = lane < HD

    def softmax_pv(parts):
        mx = None
        for s, _ in parts:
            m = jnp.max(s, axis=-1, keepdims=True)
            mx = m if mx is None else jnp.maximum(mx, m)
        den = 0.0
        acc = 0.0
        for s, vv in parts:
            e = jnp.exp(s - mx)
            den = den + jnp.sum(e, axis=-1, keepdims=True)
            acc = acc + _bdot(e, vv)
        return acc / den

    @pl.when(i == 0)
    def _():
        for p in range(HEADS // 2):
            sl = slice(2 * HD * p, 2 * HD * (p + 1))
            qp = q_ref[:, sl]
            kx = kx_ref[:, sl]
            vx = vx_ref[:, sl]
            outs = []
            for h in range(2):
                qm = jnp.where(first if h == 0 else ~first, qp, jnp.zeros_like(qp))
                outs.append(softmax_pv([(_bdot_nt(qm, kx), vx)]))
            o_ref[:, sl] = jnp.where(first, outs[0], outs[1])

    @pl.when(i > 0)
    def _():
        kbuf[0:tt, :] = kp_ref[...]
        kbuf[tt:2 * tt, :] = kc_ref[...]
        kbuf[2 * tt:3 * tt, :] = kn_ref[...]
        vbuf[0:tt, :] = vp_ref[...]
        vbuf[tt:2 * tt, :] = vc_ref[...]
        vbuf[2 * tt:3 * tt, :] = vn_ref[...]

        def body(a, carry):
            r = rpt * (i - 1) + a
            rs = jnp.clip(r - NAT_KH // 2, 0, n_rows - NAT_KH)
            off = pl.multiple_of((rs - rpt * (i - 2)) * GRID_W, GRID_W)
            d0 = rs - r + (NAT_KH - 1)
            qoff = pl.multiple_of(a * GRID_W, GRID_W)
            for p in range(HEADS // 2):
                sl = slice(2 * HD * p, 2 * HD * (p + 1))
                qp = q_ref[pl.ds(qoff, GRID_W), sl]
                kw = kbuf[pl.ds(off, nwin), sl]
                vw = vbuf[pl.ds(off, nwin), sl]
                kx = kx_ref[:, sl]
                vx = vx_ref[:, sl]
                outs = []
                for h in range(2):
                    qm = jnp.where(first if h == 0 else ~first, qp, jnp.zeros_like(qp))
                    bias = jnp.concatenate([tab_ref[2 * p + h, d0 + 2 * m] for m in range(NAT_KH // 2)], axis=1)
                    s_loc = _bdot_nt(qm, kw) + bias
                    s_ctx = _bdot_nt(qm, kx)
                    outs.append(softmax_pv([(s_loc, vw), (s_ctx, vx)]))
                o_ref[pl.ds(qoff, GRID_W), sl] = jnp.where(first, outs[0], outs[1])
            return carry

        lax.fori_loop(0, rpt, body, 0)


def _nat(qn, kn, vb, tab, tt, ctx_len):
    b, ta, c = qn.shape
    nt = ta // tt
    n_rows = (ta - ctx_len) // GRID_W
    cur = pl.BlockSpec((None, tt, c), lambda bb, i: (bb, i, 0))
    prv = pl.BlockSpec((None, tt, c), lambda bb, i: (bb, jnp.maximum(i - 1, 0), 0))
    nxt = pl.BlockSpec((None, tt, c), lambda bb, i: (bb, jnp.minimum(i + 1, nt - 1), 0))
    cx = pl.BlockSpec((None, tt, c), lambda bb, i: (bb, 0, 0))
    return pl.pallas_call(
        functools.partial(_nat_kernel, tt=tt, n_rows=n_rows),
        grid=(b, nt),
        in_specs=[cur, prv, cur, nxt, prv, cur, nxt, cx, cx,
                  pl.BlockSpec(tab.shape, lambda bb, i: (0, 0, 0, 0))],
        out_specs=pl.BlockSpec((None, tt, c), lambda bb, i: (bb, i, 0)),
        out_shape=jax.ShapeDtypeStruct((b, ta, c), F32),
        scratch_shapes=[pltpu.VMEM((3 * tt, c), BF16), pltpu.VMEM((3 * tt, c), BF16)],
        compiler_params=_params(("arbitrary",) * 2, 40 * 1024 * 1024),
        name="nat",
    )(qn, kn, kn, kn, vb, vb, vb, kn, vb, tab)


def _nat_bias_table(rpb):
    h = rpb.shape[0]
    cols = jnp.arange(GRID_W)
    cstart = jnp.clip(cols - NAT_KW // 2, 0, GRID_W - NAT_KW)
    kc = cols[None, :]
    valid = (kc >= cstart[:, None]) & (kc < cstart[:, None] + NAT_KW)
    idx = jnp.clip(kc - cols[:, None] + (NAT_KW - 1), 0, 2 * NAT_KW - 2)
    t = jnp.where(valid[None, None], rpb[:, :, idx], NEG).astype(F32)
    t = jnp.concatenate([t, jnp.full((h, 2, GRID_W, GRID_W), NEG, F32)], axis=1)
    return jnp.concatenate([t[:, :-1], t[:, 1:]], axis=-1)


def _merge_kernel(x_ref, pool_ref, of_ref, ob_ref, bon_ref, gate_ref, nat_ref, zg_ref, modc_ref, modb_ref,
                  gng_ref, gnb_ref, bd_ref, wb_ref, wo_ref, o_ref, *, tt, ctx_len):
    i = pl.program_id(1)
    bd = bd_ref[...]
    wkv = of_ref[...] + ob_ref[...]
    mu = _dot2(wkv, bd) * (1.0 / HD)
    cen = wkv - mu
    var = _dot2(cen * cen, bd) * (1.0 / HD)
    y = cen * lax.rsqrt(var + GN_EPS) * gng_ref[...] + gnb_ref[...]
    rw = (y + bon_ref[...]) * gate_ref[...]
    d = x_ref.shape[-1]
    m = jnp.zeros((tt, d), F32)
    for n, br in enumerate((pool_ref[...], rw, nat_ref[...])):
        m = m + jax.nn.sigmoid(zg_ref[:, n * d:(n + 1) * d]) * _bdot(br, wb_ref[n])
    yout = _bdot(m, wo_ref[...])
    o_ref[...] = x_ref[...] + _gate_rows(modc_ref, modb_ref, i * tt, tt, ctx_len, 2) * yout


def _merge(xa, pool, o2, bonus, gate, nat, z, modc, modb, gng, gnb, bd, wb, wo, tt, ctx_len):
    b, ta, d = xa.shape
    c = BRANCH_W
    tok = lambda w: pl.BlockSpec((None, tt, w), lambda bb, i: (bb, i, 0))
    full = lambda shape: pl.BlockSpec(shape, lambda bb, i: (0,) * len(shape))
    return pl.pallas_call(
        functools.partial(_merge_kernel, tt=tt, ctx_len=ctx_len),
        grid=(b, ta // tt),
        in_specs=[tok(d), tok(c),
                  pl.BlockSpec((None, None, tt, c), lambda bb, i: (0, bb, i, 0)),
                  pl.BlockSpec((None, None, tt, c), lambda bb, i: (1, bb, i, 0)),
                  tok(c), tok(c), tok(c),
                  pl.BlockSpec((None, tt, ZG_W), lambda bb, i: (bb, i, COL_G // ZG_W)),
                  full((6, d)), pl.BlockSpec((None, 6, d), lambda bb, i: (bb, 0, 0)),
                  full((1, c)), full((1, c)), full((c, c)), full((3, c, d)), full((d, d))],
        out_specs=tok(d),
        out_shape=jax.ShapeDtypeStruct((b, ta, d), F32),
        compiler_params=_params(("arbitrary",) * 2, 48 * 1024 * 1024),
        name="merge",
    )(xa, pool, o2, o2, bonus, gate, nat, z, modc, modb, gng, gnb, bd, wb, wo)


def _ffn_kernel(x_ref, g_ref, modc_ref, modb_ref, w1_ref, w3_ref, w2_ref, o_ref, h_scr, acc, *, tm, ctx_len):
    i = pl.program_id(1)
    f = pl.program_id(2)

    @pl.when(f == 0)
    def _():
        h_scr[...] = _norm_mod(x_ref[...], g_ref[...], modc_ref, modb_ref, i * tm, ctx_len, 3, 4).astype(BF16)
        acc[...] = jnp.zeros_like(acc)

    h = h_scr[...]
    a = jnp.dot(h, w1_ref[...], preferred_element_type=F32)
    bb = jnp.dot(h, w3_ref[...], preferred_element_type=F32)
    acc[...] += _bdot(_silu(a) * bb, w2_ref[...])

    @pl.when(f == pl.num_programs(2) - 1)
    def _():
        o_ref[...] = x_ref[...] + _gate_rows(modc_ref, modb_ref, i * tm, tm, ctx_len, 5) * acc[...]


def _ffn(xa, g, modc, modb, w1, w3, w2, tm, fc, ctx_len):
    b, ta, d = xa.shape
    dff = w1.shape[1]
    tok = pl.BlockSpec((None, tm, d), lambda bb, i, f: (bb, i, 0))
    return pl.pallas_call(
        functools.partial(_ffn_kernel, tm=tm, ctx_len=ctx_len),
        grid=(b, ta // tm, dff // fc),
        in_specs=[tok,
                  pl.BlockSpec((1, d), lambda bb, i, f: (0, 0)),
                  pl.BlockSpec((6, d), lambda bb, i, f: (0, 0)),
                  pl.BlockSpec((None, 6, d), lambda bb, i, f: (bb, 0, 0)),
                  pl.BlockSpec((d, fc), lambda bb, i, f: (0, f)),
                  pl.BlockSpec((d, fc), lambda bb, i, f: (0, f)),
                  pl.BlockSpec((fc, d), lambda bb, i, f: (f, 0))],
        out_specs=tok,
        out_shape=jax.ShapeDtypeStruct((b, ta, d), F32),
        scratch_shapes=[pltpu.VMEM((tm, d), BF16), pltpu.VMEM((tm, d), F32)],
        compiler_params=_params(("arbitrary",) * 3, 48 * 1024 * 1024),
        name="ffn",
    )(xa, g, modc, modb, w1, w3, w2)


def _moe_kernel(x_ref, g_ref, modc_ref, modb_ref, rt_ref, w1_ref, w3_ref, w2_ref, o_ref, h_scr, gate_scr, acc,
                *, tm, ctx_len, n_exp):
    i = pl.program_id(1)
    e = pl.program_id(2)
    f = pl.program_id(3)
    lane = lax.broadcasted_iota(jnp.int32, (tm, 128), 1)

    @pl.when((e == 0) & (f == 0))
    def _():
        h = _norm_mod(x_ref[...], g_ref[...], modc_ref, modb_ref, i * tm, ctx_len, 3, 4)
        h_scr[...] = h.astype(BF16)
        acc[...] = jnp.zeros_like(acc)
        logits = jnp.dot(h, rt_ref[...], precision=lax.Precision.HIGHEST, preferred_element_type=F32)
        logits = jnp.where(lane < n_exp, logits, NEG)
        m1 = jnp.max(logits, axis=-1, keepdims=True)
        i1 = jnp.min(jnp.where(logits == m1, lane, 128), axis=-1, keepdims=True)
        rest = jnp.where(lane == i1, NEG, logits)
        m2 = jnp.max(rest, axis=-1, keepdims=True)
        i2 = jnp.min(jnp.where(rest == m2, lane, 128), axis=-1, keepdims=True)
        e2 = jnp.exp(m2 - m1)
        den = 1.0 + e2
        gate_scr[...] = jnp.where(lane == i1, 1.0 / den, 0.0) + jnp.where(lane == i2, e2 / den, 0.0)

    h = h_scr[...]
    gcol = jnp.sum(jnp.where(lane == e, gate_scr[...], 0.0), axis=-1, keepdims=True)
    a = jnp.dot(h, w1_ref[...], preferred_element_type=F32)
    bb = jnp.dot(h, w3_ref[...], preferred_element_type=F32)
    acc[...] += _bdot(_silu(a) * bb * gcol, w2_ref[...])

    @pl.when((e == pl.num_programs(2) - 1) & (f == pl.num_programs(3) - 1))
    def _():
        o_ref[...] = x_ref[...] + _gate_rows(modc_ref, modb_ref, i * tm, tm, ctx_len, 5) * acc[...]


def _moe(xa, g, modc, modb, router, w1, w3, w2, tm, fc, ctx_len):
    b, ta, d = xa.shape
    n_exp, _, dff = w1.shape
    tok = pl.BlockSpec((None, tm, d), lambda bb, i, e, f: (bb, i, 0))
    return pl.pallas_call(
        functools.partial(_moe_kernel, tm=tm, ctx_len=ctx_len, n_exp=n_exp),
        grid=(b, ta // tm, n_exp, dff // fc),
        in_specs=[tok,
                  pl.BlockSpec((1, d), lambda bb, i, e, f: (0, 0)),
                  pl.BlockSpec((6, d), lambda bb, i, e, f: (0, 0)),
                  pl.BlockSpec((None, 6, d), lambda bb, i, e, f: (bb, 0, 0)),
                  pl.BlockSpec((d, 128), lambda bb, i, e, f: (0, 0)),
                  pl.BlockSpec((None, d, fc), lambda bb, i, e, f: (e, 0, f)),
                  pl.BlockSpec((None, d, fc), lambda bb, i, e, f: (e, 0, f)),
                  pl.BlockSpec((None, fc, d), lambda bb, i, e, f: (e, f, 0))],
        out_specs=tok,
        out_shape=jax.ShapeDtypeStruct((b, ta, d), F32),
        scratch_shapes=[pltpu.VMEM((tm, d), BF16), pltpu.VMEM((tm, 128), F32), pltpu.VMEM((tm, d), F32)],
        compiler_params=_params(("arbitrary",) * 4, 48 * 1024 * 1024),
        name="moe",
    )(xa, g, modc, modb, router, w1, w3, w2)


def _layer_weights(li, w_in, pool_w, pool_scale, shift_mu, decay_w0, decay_w2, iclr_a0, iclr_a2, key_kk, key_ka,
                   bonus_rk, gn_g, gn_b, gate_g2, nat_qn_g, nat_kn_g, nat_rpb, w_branch, w_out):
    c = BRANCH_W
    d = w_in.shape[1]
    wi = w_in[li]
    o_r = c
    o_lora = o_r + 3 * c
    n_lora = 2 * DECAY_LORA + 2 * ICLR_LORA + GATE_LORA
    o_q = o_lora + n_lora
    o_g = o_q + 3 * c
    pad = jnp.zeros((d, LORA_W - n_lora), F32)
    w_z = jnp.concatenate([wi[:, o_g:], wi[:, o_r:o_lora], wi[:, o_q:o_g], wi[:, :c], wi[:, o_lora:o_q], pad],
                          axis=1).astype(BF16)
    mu = shift_mu[li]
    mur = mu[:, :3 * c]
    mul = jnp.concatenate([mu[:, 3 * c:], jnp.zeros((2, LORA_W - n_lora), F32)], axis=1)
    w2f = jnp.zeros((LORA_W, 2 * c), F32)
    a2f = jnp.zeros((LORA_W, 2 * c), F32)
    for dd in range(2):
        w2f = w2f.at[dd * DECAY_LORA:(dd + 1) * DECAY_LORA, dd * c:(dd + 1) * c].set(decay_w2[li, dd])
        a2f = a2f.at[2 * DECAY_LORA + dd * ICLR_LORA:2 * DECAY_LORA + (dd + 1) * ICLR_LORA,
                     dd * c:(dd + 1) * c].set(iclr_a2[li, dd])
    o_gl = 2 * DECAY_LORA + 2 * ICLR_LORA
    g2f = jnp.zeros((LORA_W, c), F32).at[o_gl:o_gl + GATE_LORA].set(gate_g2[li])
    return dict(
        w_z=w_z, mur=mur, mul=mul,
        w0=decay_w0[li].reshape(1, 2 * c), w2f=w2f.astype(BF16),
        a0=iclr_a0[li].reshape(1, 2 * c), a2f=a2f.astype(BF16), g2f=g2f.astype(BF16),
        kkw=key_kk[li].reshape(1, c), kaw=key_ka[li].reshape(1, c), brk=bonus_rk[li].reshape(1, c),
        gng=gn_g[li].reshape(1, c), gnb=gn_b[li].reshape(1, c),
        gq=jnp.tile(nat_qn_g[li], HEADS).reshape(1, c), gk=jnp.tile(nat_kn_g[li], HEADS).reshape(1, c),
        tab=_nat_bias_table(nat_rpb[li]),
        pool_w=pool_w[li].astype(BF16), pool_scale=pool_scale[li].reshape(1, c),
        wb=w_branch[li].astype(BF16), wo=w_out[li].astype(BF16),
    )


def kernel(x, c, ctx, c_ctx, w_mod, b_mod, norm_mix_g, norm_ffn_g, w_in, pool_w, pool_scale, shift_mu, decay_w0, decay_w2, iclr_a0, iclr_a2, key_kk, key_ka, bonus_rk, gn_g, gn_b, gate_g2, nat_qn_g, nat_kn_g, nat_rpb, w_branch, w_out, ffn_w1, ffn_w3, ffn_w2, router, moe_w1, moe_w3, moe_w2):
    b, t, d = x.shape
    ctx_len = ctx.shape[1]
    depth = w_mod.shape[0]
    ta = ctx_len + t
    tt = 256
    assert ctx_len == tt and t % tt == 0 and t // GRID_W >= NAT_KH and b + 1 <= 8 and CHUNK == HD
    tm = ta // 8
    assert ta % 8 == 0 and tm % 16 == 0
    fc = 256
    n_exp = router.shape[-1]

    xa = jnp.concatenate([ctx, x], axis=1)
    cc = jnp.zeros((8, d), F32).at[:b].set(c).at[b].set(c_ctx)
    mod_all = _modulation(cc, w_mod, b_mod)
    head_id = jnp.arange(BRANCH_W) // HD
    bd = (head_id[:, None] == head_id[None, :]).astype(BF16)

    for li in range(depth):
        lw_ = _layer_weights(li, w_in, pool_w, pool_scale, shift_mu, decay_w0, decay_w2, iclr_a0, iclr_a2, key_kk,
                             key_ka, bonus_rk, gn_g, gn_b, gate_g2, nat_qn_g, nat_kn_g, nat_rpb, w_branch, w_out)
        modb = mod_all[li, :b].reshape(b, 6, d)
        modc = mod_all[li, b].reshape(6, d)
        z = _inproj(xa, norm_mix_g[li].reshape(1, d), modc, modb, lw_["w_z"], tt, ctx_len)
        pool = _pool(z, lw_["pool_w"], lw_["pool_scale"], tt, ctx_len)
        r, v, kk, lwd, kd, ka, bonus, gate = _prep(z, lw_["mur"], lw_["mul"], lw_["w0"], lw_["w2f"], lw_["a0"],
                                                   lw_["a2f"], lw_["g2f"], lw_["kkw"], lw_["kaw"], lw_["brk"], bd,
                                                   tt, ctx_len)
        o2 = _wkv(r, v, kk, lwd, kd, ka, ctx_len)
        qn, kn, vb = _qknorm(z, lw_["gq"], lw_["gk"], bd, tt)
        nat = _nat(qn, kn, vb, lw_["tab"], tt, ctx_len)
        xa = _merge(xa, pool, o2, bonus, gate, nat, z, modc, modb, lw_["gng"], lw_["gnb"], bd, lw_["wb"], lw_["wo"],
                    tt, ctx_len)
        j = li // 2
        gf = norm_ffn_g[li].reshape(1, d)
        if li % 2 == 0:
            xa = _ffn(xa, gf, modc, modb, ffn_w1[j].astype(BF16), ffn_w3[j].astype(BF16), ffn_w2[j].astype(BF16),
                      tm, fc, ctx_len)
        else:
            rt = jnp.zeros((d, 128), F32).at[:, :n_exp].set(router[j])
            xa = _moe(xa, gf, modc, modb, rt, moe_w1[j].astype(BF16), moe_w3[j].astype(BF16),
                      moe_w2[j].astype(BF16), tm, fc, ctx_len)
    return xa[:, ctx_len:]
```

```python
import functools

import jax
import jax.numpy as jnp
from jax import lax
from jax.experimental import pallas as pl
from jax.experimental.pallas import tpu as pltpu

F32 = jnp.float32
BF16 = jnp.bfloat16

GRID_W = 64
BRANCH_W = 512
POOL_WINDOWS = (2, 4, 8, 16)
POOL_GW = 128
HEADS = 8
HD = 64
DECAY_LORA = 32
ICLR_LORA = 32
GATE_LORA = 96
GN_EPS = 64e-5
NAT_KH = 8
NAT_KW = 16
NAT_SCALE = HD ** -0.5
TOP_K = 2
RMS_EPS = 1e-6
NEG = -1e30

ZG_W = 3 * 1024
COL_G = 0
COL_RKV = 3072
COL_Q = 4608
COL_K = 5120
COL_V = 5632
COL_P = 6144
COL_LORA = 6656
Z_W = 6912
LORA_W = 256

TOK_TILE = 256
CHUNK = 64
POOL_HALO = 64
V7X_VMEM_LIMIT = 56 * 1024 * 1024


def _bdot(a, b):
    return jnp.dot(a.astype(BF16), b.astype(BF16), preferred_element_type=F32)


def _bdot_nt(a, b):
    return lax.dot_general(a.astype(BF16), b.astype(BF16), (((1,), (1,)), ((), ())), preferred_element_type=F32)


def _split2(x):
    hi = x.astype(BF16)
    lo = (x - hi.astype(F32)).astype(BF16)
    return hi, lo


def _dot2(x, b):
    hi, lo = _split2(x)
    return jnp.dot(hi, b, preferred_element_type=F32) + jnp.dot(lo, b, preferred_element_type=F32)


def _ldot3(a, x):
    h1 = x.astype(BF16)
    r1 = x - h1.astype(F32)
    h2 = r1.astype(BF16)
    h3 = (r1 - h2.astype(F32)).astype(BF16)
    d = lambda h: jnp.dot(a, h, preferred_element_type=F32)
    return d(h1) + d(h2) + d(h3)


def _silu(x):
    return x * jax.nn.sigmoid(x)


def _params(sem, vmem=None):
    return pltpu.CompilerParams(dimension_semantics=sem, vmem_limit_bytes=vmem)


def _mod_kernel(c_ref, w_ref, b_ref, o_ref):
    o_ref[...] = jnp.dot(_silu(c_ref[...]), w_ref[...], precision=lax.Precision.HIGHEST,
                         preferred_element_type=F32) + b_ref[...]


def _modulation(cc, w_mod, b_mod):
    depth, d, n = w_mod.shape
    tn = 1536
    return pl.pallas_call(
        _mod_kernel,
        grid=(depth, n // tn),
        in_specs=[pl.BlockSpec((8, d), lambda l, j: (0, 0)),
                  pl.BlockSpec((None, d, tn), lambda l, j: (l, 0, j)),
                  pl.BlockSpec((None, 1, tn), lambda l, j: (l, 0, j))],
        out_specs=pl.BlockSpec((None, 8, tn), lambda l, j: (l, 0, j)),
        out_shape=jax.ShapeDtypeStruct((depth, 8, n), F32),
        compiler_params=_params(("arbitrary", "arbitrary"), 40 * 1024 * 1024),
        name="modulation",
    )(cc, w_mod, b_mod.reshape(depth, 1, n))


def _norm_mod(x, g, modc_ref, modb_ref, row0, ctx_len, k_shift, k_scale):
    rows = x.shape[0]
    isctx = (row0 + lax.broadcasted_iota(jnp.int32, (rows, 1), 0)) < ctx_len
    sh = jnp.where(isctx, modc_ref[k_shift:k_shift + 1, :], modb_ref[k_shift:k_shift + 1, :])
    sc = jnp.where(isctx, modc_ref[k_scale:k_scale + 1, :], modb_ref[k_scale:k_scale + 1, :])
    ms = jnp.mean(x * x, axis=-1, keepdims=True)
    return (x * lax.rsqrt(ms + RMS_EPS) * g) * (1.0 + sc) + sh


def _gate_rows(modc_ref, modb_ref, row0, rows, ctx_len, k_gate):
    isctx = (row0 + lax.broadcasted_iota(jnp.int32, (rows, 1), 0)) < ctx_len
    return jnp.where(isctx, modc_ref[k_gate:k_gate + 1, :], modb_ref[k_gate:k_gate + 1, :])


def _inproj_kernel(x_ref, g_ref, modc_ref, modb_ref, w_ref, z_ref, *, tt, ctx_len):
    i = pl.program_id(2)
    h = _norm_mod(x_ref[...], g_ref[...], modc_ref, modb_ref, i * tt, ctx_len, 0, 1)
    z_ref[...] = _bdot(h, w_ref[...])


def _inproj(xa, g, modc, modb, w, tt, ctx_len):
    b, ta, d = xa.shape
    nh = 2
    tn = Z_W // nh
    return pl.pallas_call(
        functools.partial(_inproj_kernel, tt=tt, ctx_len=ctx_len),
        grid=(nh, b, ta // tt),
        in_specs=[pl.BlockSpec((None, tt, d), lambda n, bb, i: (bb, i, 0)),
                  pl.BlockSpec((1, d), lambda n, bb, i: (0, 0)),
                  pl.BlockSpec((6, d), lambda n, bb, i: (0, 0)),
                  pl.BlockSpec((None, 6, d), lambda n, bb, i: (bb, 0, 0)),
                  pl.BlockSpec((d, tn), lambda n, bb, i: (0, n))],
        out_specs=pl.BlockSpec((None, tt, tn), lambda n, bb, i: (bb, i, n)),
        out_shape=jax.ShapeDtypeStruct((b, ta, Z_W), F32),
        compiler_params=_params(("arbitrary",) * 3, 48 * 1024 * 1024),
        name="inproj",
    )(xa, g, modc, modb, w)


def _pool_kernel(zp_ref, zc_ref, zn_ref, pw_ref, ps_ref, o_ref, *, tt, ctx_len, ta):
    i = pl.program_id(1)
    zc = zc_ref[...]
    zcat = jnp.concatenate([zp_ref[...], zc, zn_ref[...]], axis=0)
    t_g = i * tt + lax.broadcasted_iota(jnp.int32, (tt, 1), 0)
    s_g = i * tt - POOL_HALO + lax.broadcasted_iota(jnp.int32, (tt, tt + 2 * POOL_HALO), 1)
    in_ctx = i * tt < ctx_len
    seg_lo = jnp.where(in_ctx, 0, ctx_len)
    seg_hi = jnp.where(in_ctx, ctx_len, ta)
    for gi, win in enumerate(POOL_WINDOWS):
        sl = slice(gi * POOL_GW, (gi + 1) * POOL_GW)
        lo = jnp.clip(t_g - win // 2, seg_lo, seg_hi)
        hi = jnp.clip(t_g - win // 2 + win, seg_lo, seg_hi)
        msk = jnp.where(s_g >= lo, jnp.where(s_g < hi, 1.0, 0.0), 0.0).astype(BF16)
        hi_b, lo_b = _split2(zcat[:, sl])
        wsum = jnp.dot(msk, hi_b, preferred_element_type=F32) + jnp.dot(msk, lo_b, preferred_element_type=F32)
        p = wsum / (hi - lo).astype(F32) - zc[:, sl]
        o_ref[:, sl] = _bdot(p, pw_ref[gi]) * ps_ref[:, sl]


def _pool(z, pool_w, pool_scale, tt, ctx_len):
    b, ta, _ = z.shape
    halo = POOL_HALO
    nb = ta // halo
    cb = COL_P // BRANCH_W
    return pl.pallas_call(
        functools.partial(_pool_kernel, tt=tt, ctx_len=ctx_len, ta=ta),
        grid=(b, ta // tt),
        in_specs=[pl.BlockSpec((None, halo, BRANCH_W), lambda bb, i: (bb, jnp.maximum(i * (tt // halo) - 1, 0), cb)),
                  pl.BlockSpec((None, tt, BRANCH_W), lambda bb, i: (bb, i, cb)),
                  pl.BlockSpec((None, halo, BRANCH_W), lambda bb, i: (bb, jnp.minimum((i + 1) * (tt // halo), nb - 1), cb)),
                  pl.BlockSpec((4, POOL_GW, POOL_GW), lambda bb, i: (0, 0, 0)),
                  pl.BlockSpec((1, BRANCH_W), lambda bb, i: (0, 0))],
        out_specs=pl.BlockSpec((None, tt, BRANCH_W), lambda bb, i: (bb, i, 0)),
        out_shape=jax.ShapeDtypeStruct((b, ta, BRANCH_W), F32),
        compiler_params=_params(("arbitrary",) * 2),
        name="pool",
    )(z, z, z, pool_w, pool_scale)


def _shifted(z, zprev, znext, mu, first, last):
    tt = z.shape[0]
    ridx = lax.broadcasted_iota(jnp.int32, (tt, 1), 0)
    zp = pltpu.roll(z, 1, 0)
    zp = jnp.where(ridx == 0, zprev[7:8, :], zp)
    zp = jnp.where(first, 0.0, zp)
    zn = pltpu.roll(z, tt - 1, 0)
    zn = jnp.where(ridx == tt - 1, znext[0:1, :], zn)
    zn = jnp.where(last, 0.0, zn)
    return z + mu[0:1, :] * (zp - z) + mu[1:2, :] * (zn - z)


def _prep_kernel(rp_ref, rc_ref, rn_ref, lp_ref, lc_ref, ln_ref, mur_ref, mul_ref, w0_ref, w2_ref, a0_ref, a2_ref,
                 g2_ref, kkw_ref, kaw_ref, brk_ref, bd_ref,
                 r_ref, v_ref, kk_ref, lw_ref, kd_ref, ka_ref, bon_ref, gate_ref, *, tt, ctx_len, ta):
    i = pl.program_id(1)
    grow = i * tt + lax.broadcasted_iota(jnp.int32, (tt, 1), 0)
    first = (grow == 0) | (grow == ctx_len)
    last = (grow == ctx_len - 1) | (grow == ta - 1)
    rkv = _shifted(rc_ref[...], rp_ref[...], rn_ref[...], mur_ref[...], first, last)
    lo = _shifted(lc_ref[...], lp_ref[...], ln_ref[...], mul_ref[...], first, last)
    c = BRANCH_W
    r, k, v = rkv[:, :c], rkv[:, c:2 * c], rkv[:, 2 * c:]
    y = _bdot(jnp.tanh(lo), w2_ref[...]) + w0_ref[...]
    lw = -jnp.exp(-0.5) * jax.nn.sigmoid(y)
    a = jax.nn.sigmoid(_bdot(lo, a2_ref[...]) + a0_ref[...])
    gate_ref[...] = _bdot(jax.nn.sigmoid(lo), g2_ref[...])
    bd = bd_ref[...]
    kk = k * kkw_ref[...]
    ss = _dot2(kk * kk, bd)
    kk = kk / jnp.maximum(jnp.sqrt(ss), 1e-12)
    a_f, a_b = a[:, :c], a[:, c:]
    kaw = kaw_ref[...]
    kd_f = k * (1.0 + (a_f - 1.0) * kaw)
    kd_b = k * (1.0 + (a_b - 1.0) * kaw)
    r_ref[...] = r
    v_ref[...] = v.astype(BF16)
    kk_ref[...] = kk
    lw_ref[...] = lw
    kd_ref[:, :c] = kd_f
    kd_ref[:, c:] = kd_b
    ka_ref[:, :c] = a_f * kk
    ka_ref[:, c:] = a_b * kk
    bon_ref[...] = _dot2(r * (kd_f + kd_b) * brk_ref[...], bd) * v


def _prep(z, mur, mul, w0, w2f, a0, a2f, g2f, kkw, kaw, brk, bd, tt, ctx_len):
    b, ta, _ = z.shape
    nb = ta // 8
    c = BRANCH_W
    rb, lb = COL_RKV // (3 * c), COL_LORA // LORA_W
    prev = lambda bb, i: (bb, jnp.maximum(i * (tt // 8) - 1, 0))
    nxt = lambda bb, i: (bb, jnp.minimum((i + 1) * (tt // 8), nb - 1))
    full = lambda shape: pl.BlockSpec(shape, lambda bb, i: (0,) * len(shape))
    tok = lambda w: pl.BlockSpec((None, tt, w), lambda bb, i: (bb, i, 0))
    sds = lambda w: jax.ShapeDtypeStruct((b, ta, w), F32)
    return pl.pallas_call(
        functools.partial(_prep_kernel, tt=tt, ctx_len=ctx_len, ta=ta),
        grid=(b, ta // tt),
        in_specs=[pl.BlockSpec((None, 8, 3 * c), lambda bb, i: prev(bb, i) + (rb,)),
                  pl.BlockSpec((None, tt, 3 * c), lambda bb, i: (bb, i, rb)),
                  pl.BlockSpec((None, 8, 3 * c), lambda bb, i: nxt(bb, i) + (rb,)),
                  pl.BlockSpec((None, 8, LORA_W), lambda bb, i: prev(bb, i) + (lb,)),
                  pl.BlockSpec((None, tt, LORA_W), lambda bb, i: (bb, i, lb)),
                  pl.BlockSpec((None, 8, LORA_W), lambda bb, i: nxt(bb, i) + (lb,)),
                  full((2, 3 * c)), full((2, LORA_W)), full((1, 2 * c)), full((LORA_W, 2 * c)), full((1, 2 * c)),
                  full((LORA_W, 2 * c)), full((LORA_W, c)), full((1, c)), full((1, c)), full((1, c)), full((c, c))],
        out_specs=[tok(c), tok(c), tok(c), tok(2 * c), tok(2 * c), tok(2 * c), tok(c), tok(c)],
        out_shape=[sds(c), jax.ShapeDtypeStruct((b, ta, c), BF16), sds(c), sds(2 * c), sds(2 * c), sds(2 * c),
                   sds(c), sds(c)],
        compiler_params=_params(("arbitrary",) * 2, 40 * 1024 * 1024),
        name="rwkv_prep",
    )(z, z, z, z, z, z, mur, mul, w0, w2f, a0, a2f, g2f, kkw, kaw, brk, bd)


def _wkv_pre_kernel(r_ref, v_ref, kk_ref, lw_ref, kd_ref, ka_ref,
                    kqt_ref, rqt_ref, u0_ref, o0_ref, xt_ref, pt_ref, *, g_chunks):
    d = pl.program_id(1)
    C = CHUNK
    n = g_chunks * C
    sgn = 1 - 2 * d
    row = lax.broadcasted_iota(jnp.int32, (n, n), 0)
    col = lax.broadcasted_iota(jnp.int32, (n, n), 1)
    same = (row // C) == (col // C)
    tri = jnp.where(same, jnp.where((col - row) * sgn <= 0, 1.0, 0.0), 0.0).astype(BF16)
    lw = lw_ref[...]
    cs = _ldot3(tri, lw)
    kd = kd_ref[...]
    ka = ka_ref[...]
    kq_all = kk_ref[...] * jnp.exp(cs - lw)
    rq_all = r_ref[...] * jnp.exp(cs)
    p_inv = jnp.exp(-cs)
    kdi_all = kd * p_inv
    kai_all = ka * p_inv
    v_all = v_ref[...]

    t_i = lax.broadcasted_iota(jnp.int32, (C, 2 * C), 0)
    s_i = lax.broadcasted_iota(jnp.int32, (C, 2 * C), 1) & (C - 1)
    strict = (s_i - t_i) * sgn < 0
    incl = (s_i - t_i) * sgn <= 0
    lane = lax.broadcasted_iota(jnp.int32, (1, 2 * HD), 1)
    m1 = jnp.where(lane < HD, 1.0, 0.0)
    m2 = 1.0 - m1
    r2 = lax.broadcasted_iota(jnp.int32, (2 * HD, 2 * HD), 0)
    c2 = lax.broadcasted_iota(jnp.int32, (2 * HD, 2 * HD), 1)
    eye = jnp.where(r2 == c2, 1.0, 0.0)
    stack2 = lambda x: jnp.concatenate([x * m1, x * m2], axis=0)

    for gc in range(g_chunks):
        rows = slice(gc * C, (gc + 1) * C)
        cs_c = cs[rows]
        tot = jnp.where(d == 1, cs_c[0:1, :], cs_c[C - 1:C, :])
        p_end = jnp.exp(tot - cs_c)
        pt_ref[gc] = jnp.broadcast_to(jnp.exp(tot), (8, tot.shape[-1]))
        kde = kd[rows] * p_end
        kae = ka[rows] * p_end
        for p in range(HEADS // 2):
            sl = slice(2 * HD * p, 2 * HD * (p + 1))
            xt_ref[gc, p] = jnp.concatenate([kde[:, sl], -kae[:, sl]], axis=0).T.astype(BF16)

    units = [(slice(gc * C, (gc + 1) * C), slice(2 * HD * p, 2 * HD * (p + 1)))
             for gc in range(g_chunks) for p in range(HEADS // 2)]
    each = lambda f, *lists: [f(*a) for a in zip(*lists)]
    kq = [kq_all[rows, sl] for rows, sl in units]
    rq = [rq_all[rows, sl] for rows, sl in units]
    g = [_bdot_nt(jnp.concatenate([kq[n_], rq[n_]], axis=0),
                  jnp.concatenate([stack2(kdi_all[rows, sl]), stack2(kai_all[rows, sl])], axis=0))
         for n_, (rows, sl) in enumerate(units)]
    akk = [jnp.where(strict, x[:C, :2 * C], 0.0) for x in g]
    ark = [jnp.where(incl, x[C:, :2 * C], 0.0) for x in g]
    ara = [jnp.where(incl, x[C:, 2 * C:], 0.0) for x in g]
    lmat = [stack2(jnp.where(strict, x[:C, 2 * C:], 0.0)) for x in g]
    tm = [eye - x for x in lmat]
    pw = each(_bdot, lmat, lmat)
    n_sq = CHUNK.bit_length() - 2
    for it in range(n_sq):
        tm = each(lambda t_, p_: t_ + _bdot(t_, p_), tm, pw)
        if it + 1 < n_sq:
            pw = each(_bdot, pw, pw)
    tss = [x[:C, :] + x[C:, :] for x in tm]
    vst = [stack2(v_all[rows, sl].astype(F32)) for rows, sl in units]
    kqt = each(lambda t_, k_: _bdot(t_, stack2(k_)), tss, kq)
    akv = each(_bdot, akk, vst)
    u0 = each(lambda t_, a_: _bdot(t_, stack2(a_)), tss, akv)
    rqt = each(lambda r_, a_, k_: r_ - _bdot(a_, stack2(k_)), rq, ara, kqt)
    o0 = each(lambda ak_, aa_, v_, u_: _bdot(jnp.concatenate([ak_, -aa_], axis=1),
                                             jnp.concatenate([v_, stack2(u_)], axis=0)), ark, ara, vst, u0)
    for n_, (rows, sl) in enumerate(units):
        kqt_ref[rows, sl] = kqt[n_].astype(BF16)
        rqt_ref[rows, sl] = rqt[n_].astype(BF16)
        u0_ref[rows, sl] = u0[n_]
        o0_ref[rows, sl] = o0[n_]


def _wkv_seq_kernel(*refs, n_batch):
    ins, (of_ref, ob_ref), st_ref = refs[:14], refs[14:16], refs[16]
    ci = pl.program_id(0)
    C = CHUNK

    @pl.when(ci == 0)
    def _():
        st_ref[...] = jnp.zeros_like(st_ref)

    r2 = lax.broadcasted_iota(jnp.int32, (2 * HD, 2 * HD), 0)
    c2 = lax.broadcasted_iota(jnp.int32, (2 * HD, 2 * HD), 1)
    bdm = (r2 < HD) == (c2 < HD)
    for d in range(2):
        kqt_ref, rqt_ref, u0_ref, o0_ref, xt_ref, pt_ref, v_ref = ins[7 * d:7 * d + 7]
        o_ref = of_ref if d == 0 else ob_ref
        for bb in range(n_batch):
            for p in range(HEADS // 2):
                sl = slice(2 * HD * p, 2 * HD * (p + 1))
                st = st_ref[d, bb, p]
                lhs = jnp.concatenate([kqt_ref[bb, :, sl], rqt_ref[bb, :, sl]], axis=0)
                uo = jnp.dot(lhs, st.astype(BF16), preferred_element_type=F32)
                u = uo[:C, :] + u0_ref[bb, :, sl]
                o_ref[bb, :, sl] = uo[C:, :] + o0_ref[bb, :, sl]
                yv = jnp.concatenate([v_ref[bb, :, sl], u.astype(BF16)], axis=0)
                upd = jnp.dot(xt_ref[bb, p], yv, preferred_element_type=F32)
                pcol = jnp.broadcast_to(pt_ref[bb, 0:1, sl], (2 * HD, 2 * HD)).T
                st_ref[d, bb, p] = st * pcol + jnp.where(bdm, upd, 0.0)


def _wkv(r, v, kk, lw, kd, ka, ctx_len):
    b, ta, c = r.shape
    nch = ta // CHUNK
    ncc = ctx_len // CHUNK
    gch = 4
    assert nch % gch == 0
    n = gch * CHUNK
    npair = HEADS // 2
    shared = pl.BlockSpec((None, n, c), lambda bb, d, i: (bb, i, 0))
    per_dir = pl.BlockSpec((None, n, c), lambda bb, d, i: (bb, i, d))
    tok = pl.BlockSpec((None, None, n, c), lambda bb, d, i: (d, bb, i, 0))
    kqt, rqt, u0, o0, xt, pt = pl.pallas_call(
        functools.partial(_wkv_pre_kernel, g_chunks=gch),
        grid=(b, 2, nch // gch),
        in_specs=[shared, shared, shared, per_dir, per_dir, per_dir],
        out_specs=[tok, tok, tok, tok,
                   pl.BlockSpec((None, None, gch, npair, 2 * HD, 2 * CHUNK), lambda bb, d, i: (d, bb, i, 0, 0, 0)),
                   pl.BlockSpec((None, None, gch, 8, c), lambda bb, d, i: (d, bb, i, 0, 0))],
        out_shape=[jax.ShapeDtypeStruct((2, b, ta, c), BF16), jax.ShapeDtypeStruct((2, b, ta, c), BF16),
                   jax.ShapeDtypeStruct((2, b, ta, c), F32), jax.ShapeDtypeStruct((2, b, ta, c), F32),
                   jax.ShapeDtypeStruct((2, b, nch, npair, 2 * HD, 2 * CHUNK), BF16),
                   jax.ShapeDtypeStruct((2, b, nch, 8, c), F32)],
        compiler_params=_params(("arbitrary",) * 3, 40 * 1024 * 1024),
        name="wkv_pre",
    )(r, v, kk, lw, kd, ka)

    def chunk(d, ci):
        return ci if d == 0 else jnp.where(ci < ncc, ncc - 1 - ci, nch + ncc - 1 - ci)

    in_specs, args = [], []
    for d in range(2):
        tokd = pl.BlockSpec((None, b, CHUNK, c), lambda ci, d=d: (d, 0, chunk(d, ci), 0))
        in_specs += [tokd, tokd, tokd, tokd,
                     pl.BlockSpec((None, b, None, npair, 2 * HD, 2 * CHUNK),
                                  lambda ci, d=d: (d, 0, chunk(d, ci), 0, 0, 0)),
                     pl.BlockSpec((None, b, None, 8, c), lambda ci, d=d: (d, 0, chunk(d, ci), 0, 0)),
                     pl.BlockSpec((b, CHUNK, c), lambda ci, d=d: (0, chunk(d, ci), 0))]
        args += [kqt, rqt, u0, o0, xt, pt, v]
    return pl.pallas_call(
        functools.partial(_wkv_seq_kernel, n_batch=b),
        grid=(nch,),
        in_specs=in_specs,
        out_specs=[pl.BlockSpec((b, CHUNK, c), lambda ci, d=d: (0, chunk(d, ci), 0)) for d in range(2)],
        out_shape=[jax.ShapeDtypeStruct((b, ta, c), F32)] * 2,
        scratch_shapes=[pltpu.VMEM((2, b, npair, 2 * HD, 2 * HD), F32)],
        compiler_params=_params(("arbitrary",)),
        name="wkv_seq",
    )(*args)


def _qknorm_kernel(q_ref, k_ref, v_ref, gq_ref, gk_ref, bd_ref, qo_ref, ko_ref, vo_ref):
    bd = bd_ref[...]

    def norm(x, g):
        ms = _dot2(x * x, bd) * (1.0 / HD)
        return x * lax.rsqrt(ms + RMS_EPS) * g

    qo_ref[...] = (norm(q_ref[...], gq_ref[...]) * NAT_SCALE).astype(BF16)
    ko_ref[...] = norm(k_ref[...], gk_ref[...]).astype(BF16)
    vo_ref[...] = v_ref[...].astype(BF16)


def _qknorm(z, gq, gk, bd, tt):
    b, ta, _ = z.shape
    c = BRANCH_W
    col = lambda cb: pl.BlockSpec((None, tt, c), lambda bb, i: (bb, i, cb))
    full = lambda shape: pl.BlockSpec(shape, lambda bb, i: (0,) * len(shape))
    out = pl.BlockSpec((None, tt, c), lambda bb, i: (bb, i, 0))
    sds = jax.ShapeDtypeStruct((b, ta, c), BF16)
    return pl.pallas_call(
        _qknorm_kernel,
        grid=(b, ta // tt),
        in_specs=[col(COL_Q // c), col(COL_K // c), col(COL_V // c), full((1, c)), full((1, c)), full((c, c))],
        out_specs=[out, out, out],
        out_shape=[sds, sds, sds],
        compiler_params=_params(("arbitrary",) * 2),
        name="nat_qknorm",
    )(z, z, z, gq, gk, bd)


def _nat_kernel(q_ref, kp_ref, kc_ref, kn_ref, vp_ref, vc_ref, vn_ref, kx_ref, vx_ref, bias_ref, o_ref,
                kbuf, vbuf, *, tt):
    i = pl.program_id(1)
    lane = lax.broadcasted_iota(jnp.int32, (1, 2 * HD), 1)
    first = lane < HD

    def softmax_pv(parts):
        mx = None
        for s, _ in parts:
            m = jnp.max(s, axis=-1, keepdims=True)
            mx = m if mx is None else jnp.maximum(mx, m)
        den = 0.0
        acc = 0.0
        for s, vv in parts:
            e = jnp.exp(s - mx)
            den = den + jnp.sum(e, axis=-1, keepdims=True)
            acc = acc + _bdot(e, vv)
        return acc / den

    @pl.when(i == 0)
    def _():
        for p in range(HEADS // 2):
            sl = slice(2 * HD * p, 2 * HD * (p + 1))
            qp = q_ref[:, sl]
            kx = kx_ref[:, sl]
            vx = vx_ref[:, sl]
            outs = []
            for h in range(2):
                qm = jnp.where(first if h == 0 else ~first, qp, jnp.zeros_like(qp))
                outs.append(softmax_pv([(_bdot_nt(qm, kx), vx)]))
            o_ref[:, sl] = jnp.where(first, outs[0], outs[1])

    @pl.when(i > 0)
    def _():
        for n_, (kr, vr) in enumerate(((kp_ref, vp_ref), (kc_ref, vc_ref), (kn_ref, vn_ref), (kx_ref, vx_ref))):
            kbuf[n_ * tt:(n_ + 1) * tt, :] = kr[...]
            vbuf[n_ * tt:(n_ + 1) * tt, :] = vr[...]

        def scores(hd):
            sl = slice(2 * HD * (hd // 2), 2 * HD * (hd // 2 + 1))
            qp = q_ref[:, sl]
            qm = jnp.where(first if hd % 2 == 0 else ~first, qp, jnp.zeros_like(qp))
            return _bdot_nt(qm, kbuf[:, sl])

        s_next = scores(0)
        outs = []
        for hd in range(HEADS):
            s = s_next
            if hd + 1 < HEADS:
                s_next = scores(hd + 1)
            sl = slice(2 * HD * (hd // 2), 2 * HD * (hd // 2 + 1))
            s_loc = s[:, :3 * tt] + bias_ref[hd]
            s_ctx = s[:, 3 * tt:]
            mx = jnp.maximum(jnp.max(s_loc, axis=-1, keepdims=True), jnp.max(s_ctx, axis=-1, keepdims=True))
            e = jnp.concatenate([jnp.exp(s_loc - mx), jnp.exp(s_ctx - mx)], axis=1)
            den = jnp.sum(e, axis=-1, keepdims=True)
            outs.append(_bdot(e, vbuf[:, sl]) / den)
            if hd % 2 == 1:
                o_ref[:, sl] = jnp.where(first, outs[hd - 1], outs[hd])


def _nat(qn, kn, vb, bias3, tt, ctx_len):
    b, ta, c = qn.shape
    nt = ta // tt
    cur = pl.BlockSpec((None, tt, c), lambda bb, i: (bb, i, 0))
    prv = pl.BlockSpec((None, tt, c), lambda bb, i: (bb, jnp.maximum(i - 1, 0), 0))
    nxt = pl.BlockSpec((None, tt, c), lambda bb, i: (bb, jnp.minimum(i + 1, nt - 1), 0))
    cx = pl.BlockSpec((None, tt, c), lambda bb, i: (bb, 0, 0))
    variant = lambda i: jnp.where(i <= 1, 0, jnp.where(i == nt - 1, 2, 1))
    return pl.pallas_call(
        functools.partial(_nat_kernel, tt=tt),
        grid=(b, nt),
        in_specs=[cur, prv, cur, nxt, prv, cur, nxt, cx, cx,
                  pl.BlockSpec((None, HEADS, tt, 3 * tt), lambda bb, i: (variant(i), 0, 0, 0))],
        out_specs=pl.BlockSpec((None, tt, c), lambda bb, i: (bb, i, 0)),
        out_shape=jax.ShapeDtypeStruct((b, ta, c), F32),
        scratch_shapes=[pltpu.VMEM((4 * tt, c), BF16), pltpu.VMEM((4 * tt, c), BF16)],
        compiler_params=_params(("arbitrary",) * 2, 48 * 1024 * 1024),
        name="nat",
    )(qn, kn, kn, kn, vb, vb, vb, kn, vb, bias3)


def _nat_bias_table(rpb, tt):
    h = rpb.shape[0]
    rpt = tt // GRID_W
    cols = jnp.arange(GRID_W)
    cstart = jnp.clip(cols - NAT_KW // 2, 0, GRID_W - NAT_KW)
    kc = cols[None, :]
    col_ok = (kc >= cstart[:, None]) & (kc < cstart[:, None] + NAT_KW)
    idx = jnp.clip(kc - cols[:, None] + (NAT_KW - 1), 0, 2 * NAT_KW - 2)
    t15 = jnp.where(col_ok[None, None], rpb[:, :, idx], NEG).astype(F32)
    a = jnp.arange(rpt)[:, None]
    j = jnp.arange(3 * rpt)[None, :]
    d = j - a - rpt + (NAT_KH - 1)
    blocks = t15[:, d]
    lo = jnp.stack([jnp.full((rpt,), rpt), a[:, 0] + rpt - NAT_KH // 2, jnp.full((rpt,), 2 * rpt - NAT_KH)])
    row_ok = (j[None] >= lo[:, :, None]) & (j[None] < lo[:, :, None] + NAT_KH)
    full = jnp.where(row_ok[:, None, :, :, None, None], blocks[None], NEG)
    return full.transpose(0, 1, 2, 4, 3, 5).reshape(3, h, tt, 3 * tt)


def _merge_kernel(x_ref, pool_ref, of_ref, ob_ref, bon_ref, gate_ref, nat_ref, zg_ref, modc_ref, modb_ref,
                  gng_ref, gnb_ref, bd_ref, wb_ref, wo_ref, o_ref, *, tt, ctx_len):
    i = pl.program_id(1)
    bd = bd_ref[...]
    wkv = of_ref[...] + ob_ref[...]
    mu = _dot2(wkv, bd) * (1.0 / HD)
    cen = wkv - mu
    var = _dot2(cen * cen, bd) * (1.0 / HD)
    y = cen * lax.rsqrt(var + GN_EPS) * gng_ref[...] + gnb_ref[...]
    rw = (y + bon_ref[...]) * gate_ref[...]
    d = x_ref.shape[-1]
    m = jnp.zeros((tt, d), F32)
    for n, br in enumerate((pool_ref[...], rw, nat_ref[...])):
        m = m + jax.nn.sigmoid(zg_ref[:, n * d:(n + 1) * d]) * _bdot(br, wb_ref[n])
    yout = _bdot(m, wo_ref[...])
    o_ref[...] = x_ref[...] + _gate_rows(modc_ref, modb_ref, i * tt, tt, ctx_len, 2) * yout


def _merge(xa, pool, o2, bonus, gate, nat, z, modc, modb, gng, gnb, bd, wb, wo, tt, ctx_len):
    b, ta, d = xa.shape
    c = BRANCH_W
    tok = lambda w: pl.BlockSpec((None, tt, w), lambda bb, i: (bb, i, 0))
    full = lambda shape: pl.BlockSpec(shape, lambda bb, i: (0,) * len(shape))
    return pl.pallas_call(
        functools.partial(_merge_kernel, tt=tt, ctx_len=ctx_len),
        grid=(b, ta // tt),
        in_specs=[tok(d), tok(c), tok(c), tok(c), tok(c), tok(c), tok(c),
                  pl.BlockSpec((None, tt, ZG_W), lambda bb, i: (bb, i, COL_G // ZG_W)),
                  full((6, d)), pl.BlockSpec((None, 6, d), lambda bb, i: (bb, 0, 0)),
                  full((1, c)), full((1, c)), full((c, c)), full((3, c, d)), full((d, d))],
        out_specs=tok(d),
        out_shape=jax.ShapeDtypeStruct((b, ta, d), F32),
        compiler_params=_params(("arbitrary",) * 2, 48 * 1024 * 1024),
        name="merge",
    )(xa, pool, o2[0], o2[1], bonus, gate, nat, z, modc, modb, gng, gnb, bd, wb, wo)


def _ffn_kernel(x_ref, g_ref, modc_ref, modb_ref, w1_ref, w3_ref, w2_ref, o_ref, h_scr, acc, *, tm, ctx_len):
    i = pl.program_id(1)
    f = pl.program_id(2)

    @pl.when(f == 0)
    def _():
        h_scr[...] = _norm_mod(x_ref[...], g_ref[...], modc_ref, modb_ref, i * tm, ctx_len, 3, 4).astype(BF16)
        acc[...] = jnp.zeros_like(acc)

    h = h_scr[...]
    a = jnp.dot(h, w1_ref[...], preferred_element_type=F32)
    bb = jnp.dot(h, w3_ref[...], preferred_element_type=F32)
    acc[...] += _bdot(_silu(a) * bb, w2_ref[...])

    @pl.when(f == pl.num_programs(2) - 1)
    def _():
        o_ref[...] = x_ref[...] + _gate_rows(modc_ref, modb_ref, i * tm, tm, ctx_len, 5) * acc[...]


def _ffn(xa, g, modc, modb, w1, w3, w2, tm, fc, ctx_len):
    b, ta, d = xa.shape
    dff = w1.shape[1]
    tok = pl.BlockSpec((None, tm, d), lambda bb, i, f: (bb, i, 0))
    return pl.pallas_call(
        functools.partial(_ffn_kernel, tm=tm, ctx_len=ctx_len),
        grid=(b, ta // tm, dff // fc),
        in_specs=[tok,
                  pl.BlockSpec((1, d), lambda bb, i, f: (0, 0)),
                  pl.BlockSpec((6, d), lambda bb, i, f: (0, 0)),
                  pl.BlockSpec((None, 6, d), lambda bb, i, f: (bb, 0, 0)),
                  pl.BlockSpec((d, fc), lambda bb, i, f: (0, f)),
                  pl.BlockSpec((d, fc), lambda bb, i, f: (0, f)),
                  pl.BlockSpec((fc, d), lambda bb, i, f: (f, 0))],
        out_specs=tok,
        out_shape=jax.ShapeDtypeStruct((b, ta, d), F32),
        scratch_shapes=[pltpu.VMEM((tm, d), BF16), pltpu.VMEM((tm, d), F32)],
        compiler_params=_params(("arbitrary",) * 3, 48 * 1024 * 1024),
        name="ffn",
    )(xa, g, modc, modb, w1, w3, w2)


def _moe_kernel(x_ref, g_ref, modc_ref, modb_ref, rt_ref, w1_ref, w3_ref, w2_ref, o_ref, h_scr, gate_scr, acc,
                *, tm, ctx_len, n_exp):
    i = pl.program_id(1)
    e = pl.program_id(2)
    f = pl.program_id(3)
    lane = lax.broadcasted_iota(jnp.int32, (tm, 128), 1)

    @pl.when((e == 0) & (f == 0))
    def _():
        h = _norm_mod(x_ref[...], g_ref[...], modc_ref, modb_ref, i * tm, ctx_len, 3, 4)
        h_scr[...] = h.astype(BF16)
        acc[...] = jnp.zeros_like(acc)
        logits = jnp.dot(h, rt_ref[...], precision=lax.Precision.HIGHEST, preferred_element_type=F32)
        logits = jnp.where(lane < n_exp, logits, NEG)
        m1 = jnp.max(logits, axis=-1, keepdims=True)
        i1 = jnp.min(jnp.where(logits == m1, lane, 128), axis=-1, keepdims=True)
        rest = jnp.where(lane == i1, NEG, logits)
        m2 = jnp.max(rest, axis=-1, keepdims=True)
        i2 = jnp.min(jnp.where(rest == m2, lane, 128), axis=-1, keepdims=True)
        e2 = jnp.exp(m2 - m1)
        den = 1.0 + e2
        gate_scr[...] = jnp.where(lane == i1, 1.0 / den, 0.0) + jnp.where(lane == i2, e2 / den, 0.0)

    h = h_scr[...]
    gcol = jnp.sum(jnp.where(lane == e, gate_scr[...], 0.0), axis=-1, keepdims=True)
    a = jnp.dot(h, w1_ref[...], preferred_element_type=F32)
    bb = jnp.dot(h, w3_ref[...], preferred_element_type=F32)
    acc[...] += _bdot(_silu(a) * bb * gcol, w2_ref[...])

    @pl.when((e == pl.num_programs(2) - 1) & (f == pl.num_programs(3) - 1))
    def _():
        o_ref[...] = x_ref[...] + _gate_rows(modc_ref, modb_ref, i * tm, tm, ctx_len, 5) * acc[...]


def _moe(xa, g, modc, modb, router, w1, w3, w2, tm, fc, ctx_len):
    b, ta, d = xa.shape
    n_exp, _, dff = w1.shape
    tok = pl.BlockSpec((None, tm, d), lambda bb, i, e, f: (bb, i, 0))
    return pl.pallas_call(
        functools.partial(_moe_kernel, tm=tm, ctx_len=ctx_len, n_exp=n_exp),
        grid=(b, ta // tm, n_exp, dff // fc),
        in_specs=[tok,
                  pl.BlockSpec((1, d), lambda bb, i, e, f: (0, 0)),
                  pl.BlockSpec((6, d), lambda bb, i, e, f: (0, 0)),
                  pl.BlockSpec((None, 6, d), lambda bb, i, e, f: (bb, 0, 0)),
                  pl.BlockSpec((d, 128), lambda bb, i, e, f: (0, 0)),
                  pl.BlockSpec((None, d, fc), lambda bb, i, e, f: (e, 0, f)),
                  pl.BlockSpec((None, d, fc), lambda bb, i, e, f: (e, 0, f)),
                  pl.BlockSpec((None, fc, d), lambda bb, i, e, f: (e, f, 0))],
        out_specs=tok,
        out_shape=jax.ShapeDtypeStruct((b, ta, d), F32),
        scratch_shapes=[pltpu.VMEM((tm, d), BF16), pltpu.VMEM((tm, 128), F32), pltpu.VMEM((tm, d), F32)],
        compiler_params=_params(("arbitrary",) * 4, 48 * 1024 * 1024),
        name="moe",
    )(xa, g, modc, modb, router, w1, w3, w2)


def _layer_weights(li, w_in, pool_w, pool_scale, shift_mu, decay_w0, decay_w2, iclr_a0, iclr_a2, key_kk, key_ka,
                   bonus_rk, gn_g, gn_b, gate_g2, nat_qn_g, nat_kn_g, nat_rpb, w_branch, w_out):
    c = BRANCH_W
    d = w_in.shape[1]
    wi = w_in[li]
    o_r = c
    o_lora = o_r + 3 * c
    n_lora = 2 * DECAY_LORA + 2 * ICLR_LORA + GATE_LORA
    o_q = o_lora + n_lora
    o_g = o_q + 3 * c
    pad = jnp.zeros((d, LORA_W - n_lora), F32)
    w_z = jnp.concatenate([wi[:, o_g:], wi[:, o_r:o_lora], wi[:, o_q:o_g], wi[:, :c], wi[:, o_lora:o_q], pad],
                          axis=1).astype(BF16)
    mu = shift_mu[li]
    mur = mu[:, :3 * c]
    mul = jnp.concatenate([mu[:, 3 * c:], jnp.zeros((2, LORA_W - n_lora), F32)], axis=1)
    w2f = jnp.zeros((LORA_W, 2 * c), F32)
    a2f = jnp.zeros((LORA_W, 2 * c), F32)
    for dd in range(2):
        w2f = w2f.at[dd * DECAY_LORA:(dd + 1) * DECAY_LORA, dd * c:(dd + 1) * c].set(decay_w2[li, dd])
        a2f = a2f.at[2 * DECAY_LORA + dd * ICLR_LORA:2 * DECAY_LORA + (dd + 1) * ICLR_LORA,
                     dd * c:(dd + 1) * c].set(iclr_a2[li, dd])
    o_gl = 2 * DECAY_LORA + 2 * ICLR_LORA
    g2f = jnp.zeros((LORA_W, c), F32).at[o_gl:o_gl + GATE_LORA].set(gate_g2[li])
    return dict(
        w_z=w_z, mur=mur, mul=mul,
        w0=decay_w0[li].reshape(1, 2 * c), w2f=w2f.astype(BF16),
        a0=iclr_a0[li].reshape(1, 2 * c), a2f=a2f.astype(BF16), g2f=g2f.astype(BF16),
        kkw=key_kk[li].reshape(1, c), kaw=key_ka[li].reshape(1, c), brk=bonus_rk[li].reshape(1, c),
        gng=gn_g[li].reshape(1, c), gnb=gn_b[li].reshape(1, c),
        gq=jnp.tile(nat_qn_g[li], HEADS).reshape(1, c), gk=jnp.tile(nat_kn_g[li], HEADS).reshape(1, c),
        tab=_nat_bias_table(nat_rpb[li], TOK_TILE),
        pool_w=pool_w[li].astype(BF16), pool_scale=pool_scale[li].reshape(1, c),
        wb=w_branch[li].astype(BF16), wo=w_out[li].astype(BF16),
    )


def kernel(x, c, ctx, c_ctx, w_mod, b_mod, norm_mix_g, norm_ffn_g, w_in, pool_w, pool_scale, shift_mu, decay_w0, decay_w2, iclr_a0, iclr_a2, key_kk, key_ka, bonus_rk, gn_g, gn_b, gate_g2, nat_qn_g, nat_kn_g, nat_rpb, w_branch, w_out, ffn_w1, ffn_w3, ffn_w2, router, moe_w1, moe_w3, moe_w2):
    b, t, d = x.shape
    ctx_len = ctx.shape[1]
    depth = w_mod.shape[0]
    ta = ctx_len + t
    tt = TOK_TILE
    assert ctx_len == tt and t % tt == 0 and t // GRID_W >= NAT_KH and b + 1 <= 8 and CHUNK == HD
    tm = ta // 8
    assert ta % 8 == 0 and tm % 16 == 0
    fc = 256
    n_exp = router.shape[-1]

    xa = jnp.concatenate([ctx, x], axis=1)
    cc = jnp.zeros((8, d), F32).at[:b].set(c).at[b].set(c_ctx)
    mod_all = _modulation(cc, w_mod, b_mod)
    head_id = jnp.arange(BRANCH_W) // HD
    bd = (head_id[:, None] == head_id[None, :]).astype(BF16)

    for li in range(depth):
        lw_ = _layer_weights(li, w_in, pool_w, pool_scale, shift_mu, decay_w0, decay_w2, iclr_a0, iclr_a2, key_kk,
                             key_ka, bonus_rk, gn_g, gn_b, gate_g2, nat_qn_g, nat_kn_g, nat_rpb, w_branch, w_out)
        modb = mod_all[li, :b].reshape(b, 6, d)
        modc = mod_all[li, b].reshape(6, d)
        z = _inproj(xa, norm_mix_g[li].reshape(1, d), modc, modb, lw_["w_z"], tt, ctx_len)
        pool = _pool(z, lw_["pool_w"], lw_["pool_scale"], tt, ctx_len)
        r, v, kk, lwd, kd, ka, bonus, gate = _prep(z, lw_["mur"], lw_["mul"], lw_["w0"], lw_["w2f"], lw_["a0"],
                                                   lw_["a2f"], lw_["g2f"], lw_["kkw"], lw_["kaw"], lw_["brk"], bd,
                                                   tt, ctx_len)
        o2 = _wkv(r, v, kk, lwd, kd, ka, ctx_len)
        qn, kn, vb = _qknorm(z, lw_["gq"], lw_["gk"], bd, tt)
        nat = _nat(qn, kn, vb, lw_["tab"], tt, ctx_len)
        xa = _merge(xa, pool, o2, bonus, gate, nat, z, modc, modb, lw_["gng"], lw_["gnb"], bd, lw_["wb"], lw_["wo"],
                    tt, ctx_len)
        j = li // 2
        gf = norm_ffn_g[li].reshape(1, d)
        if li % 2 == 0:
            xa = _ffn(xa, gf, modc, modb, ffn_w1[j].astype(BF16), ffn_w3[j].astype(BF16), ffn_w2[j].astype(BF16),
                      tm, fc, ctx_len)
        else:
            rt = jnp.zeros((d, 128), F32).at[:, :n_exp].set(router[j])
            xa = _moe(xa, gf, modc, modb, rt, moe_w1[j].astype(BF16), moe_w3[j].astype(BF16),
                      moe_w2[j].astype(BF16), tm, fc, ctx_len)
    return xa[:, ctx_len:]
```

```python
import functools

import jax
import jax.numpy as jnp
from jax import lax
from jax.experimental import pallas as pl
from jax.experimental.pallas import tpu as pltpu

F32 = jnp.float32
BF16 = jnp.bfloat16

GRID_W = 64
BRANCH_W = 512
POOL_WINDOWS = (2, 4, 8, 16)
POOL_GW = 128
HEADS = 8
HD = 64
DECAY_LORA = 32
ICLR_LORA = 32
GATE_LORA = 96
GN_EPS = 64e-5
NAT_KH = 8
NAT_KW = 16
NAT_SCALE = HD ** -0.5
TOP_K = 2
RMS_EPS = 1e-6
NEG = -1e30

ZG_W = 3 * 1024
COL_G = 0
COL_RKV = 3072
COL_Q = 4608
COL_K = 5120
COL_V = 5632
COL_P = 6144
COL_LORA = 6656
Z_W = 6912
LORA_W = 256

TOK_TILE = 256
CHUNK = 64
MOE_TILE = 512
POOL_HALO = 64
V7X_VMEM_LIMIT = 56 * 1024 * 1024


def _bdot(a, b):
    return jnp.dot(a.astype(BF16), b.astype(BF16), preferred_element_type=F32)


def _bdot_nt(a, b):
    return lax.dot_general(a.astype(BF16), b.astype(BF16), (((1,), (1,)), ((), ())), preferred_element_type=F32)


def _split2(x):
    hi = x.astype(BF16)
    lo = (x - hi.astype(F32)).astype(BF16)
    return hi, lo


def _dot2(x, b):
    hi, lo = _split2(x)
    return jnp.dot(hi, b, preferred_element_type=F32) + jnp.dot(lo, b, preferred_element_type=F32)


def _ldot3(a, x):
    h1 = x.astype(BF16)
    r1 = x - h1.astype(F32)
    h2 = r1.astype(BF16)
    h3 = (r1 - h2.astype(F32)).astype(BF16)
    d = lambda h: jnp.dot(a, h, preferred_element_type=F32)
    return d(h1) + d(h2) + d(h3)


def _silu(x):
    return x * jax.nn.sigmoid(x)


def _params(sem, vmem=None):
    return pltpu.CompilerParams(dimension_semantics=sem, vmem_limit_bytes=vmem)


def _mod_kernel(c_ref, w_ref, b_ref, o_ref):
    o_ref[...] = jnp.dot(_silu(c_ref[...]), w_ref[...], precision=lax.Precision.HIGHEST,
                         preferred_element_type=F32) + b_ref[...]


def _modulation(cc, w_mod, b_mod):
    depth, d, n = w_mod.shape
    tn = 1536
    return pl.pallas_call(
        _mod_kernel,
        grid=(depth, n // tn),
        in_specs=[pl.BlockSpec((8, d), lambda l, j: (0, 0)),
                  pl.BlockSpec((None, d, tn), lambda l, j: (l, 0, j)),
                  pl.BlockSpec((None, 1, tn), lambda l, j: (l, 0, j))],
        out_specs=pl.BlockSpec((None, 8, tn), lambda l, j: (l, 0, j)),
        out_shape=jax.ShapeDtypeStruct((depth, 8, n), F32),
        compiler_params=_params(("arbitrary", "arbitrary"), 40 * 1024 * 1024),
        name="modulation",
    )(cc, w_mod, b_mod.reshape(depth, 1, n))


def _norm_mod(x, g, modc_ref, modb_ref, row0, ctx_len, k_shift, k_scale):
    rows = x.shape[0]
    isctx = (row0 + lax.broadcasted_iota(jnp.int32, (rows, 1), 0)) < ctx_len
    sh = jnp.where(isctx, modc_ref[k_shift:k_shift + 1, :], modb_ref[k_shift:k_shift + 1, :])
    sc = jnp.where(isctx, modc_ref[k_scale:k_scale + 1, :], modb_ref[k_scale:k_scale + 1, :])
    ms = jnp.mean(x * x, axis=-1, keepdims=True)
    return (x * lax.rsqrt(ms + RMS_EPS) * g) * (1.0 + sc) + sh


def _gate_rows(modc_ref, modb_ref, row0, rows, ctx_len, k_gate):
    isctx = (row0 + lax.broadcasted_iota(jnp.int32, (rows, 1), 0)) < ctx_len
    return jnp.where(isctx, modc_ref[k_gate:k_gate + 1, :], modb_ref[k_gate:k_gate + 1, :])


def _inproj_kernel(x_ref, g_ref, modc_ref, modb_ref, w_ref, z_ref, *, tt, ctx_len):
    i = pl.program_id(2)
    h = _norm_mod(x_ref[...], g_ref[...], modc_ref, modb_ref, i * tt, ctx_len, 0, 1)
    z_ref[...] = _bdot(h, w_ref[...])


def _inproj(xa, g, modc, modb, w, tt, ctx_len):
    b, ta, d = xa.shape
    nh = 2
    tn = Z_W // nh
    return pl.pallas_call(
        functools.partial(_inproj_kernel, tt=tt, ctx_len=ctx_len),
        grid=(nh, b, ta // tt),
        in_specs=[pl.BlockSpec((None, tt, d), lambda n, bb, i: (bb, i, 0)),
                  pl.BlockSpec((1, d), lambda n, bb, i: (0, 0)),
                  pl.BlockSpec((6, d), lambda n, bb, i: (0, 0)),
                  pl.BlockSpec((None, 6, d), lambda n, bb, i: (bb, 0, 0)),
                  pl.BlockSpec((d, tn), lambda n, bb, i: (0, n))],
        out_specs=pl.BlockSpec((None, tt, tn), lambda n, bb, i: (bb, i, n)),
        out_shape=jax.ShapeDtypeStruct((b, ta, Z_W), F32),
        compiler_params=_params(("arbitrary",) * 3, 48 * 1024 * 1024),
        name="inproj",
    )(xa, g, modc, modb, w)


def _pool_kernel(zp_ref, zc_ref, zn_ref, pw_ref, ps_ref, o_ref, *, tt, ctx_len, ta):
    i = pl.program_id(1)
    zc = zc_ref[...]
    zcat = jnp.concatenate([zp_ref[...], zc, zn_ref[...]], axis=0)
    t_g = i * tt + lax.broadcasted_iota(jnp.int32, (tt, 1), 0)
    s_g = i * tt - POOL_HALO + lax.broadcasted_iota(jnp.int32, (tt, tt + 2 * POOL_HALO), 1)
    in_ctx = i * tt < ctx_len
    seg_lo = jnp.where(in_ctx, 0, ctx_len)
    seg_hi = jnp.where(in_ctx, ctx_len, ta)
    for gi, win in enumerate(POOL_WINDOWS):
        sl = slice(gi * POOL_GW, (gi + 1) * POOL_GW)
        lo = jnp.clip(t_g - win // 2, seg_lo, seg_hi)
        hi = jnp.clip(t_g - win // 2 + win, seg_lo, seg_hi)
        msk = jnp.where(s_g >= lo, jnp.where(s_g < hi, 1.0, 0.0), 0.0).astype(BF16)
        hi_b, lo_b = _split2(zcat[:, sl])
        wsum = jnp.dot(msk, hi_b, preferred_element_type=F32) + jnp.dot(msk, lo_b, preferred_element_type=F32)
        p = wsum / (hi - lo).astype(F32) - zc[:, sl]
        o_ref[:, sl] = _bdot(p, pw_ref[gi]) * ps_ref[:, sl]


def _pool(z, pool_w, pool_scale, tt, ctx_len):
    b, ta, _ = z.shape
    halo = POOL_HALO
    nb = ta // halo
    cb = COL_P // BRANCH_W
    return pl.pallas_call(
        functools.partial(_pool_kernel, tt=tt, ctx_len=ctx_len, ta=ta),
        grid=(b, ta // tt),
        in_specs=[pl.BlockSpec((None, halo, BRANCH_W), lambda bb, i: (bb, jnp.maximum(i * (tt // halo) - 1, 0), cb)),
                  pl.BlockSpec((None, tt, BRANCH_W), lambda bb, i: (bb, i, cb)),
                  pl.BlockSpec((None, halo, BRANCH_W), lambda bb, i: (bb, jnp.minimum((i + 1) * (tt // halo), nb - 1), cb)),
                  pl.BlockSpec((4, POOL_GW, POOL_GW), lambda bb, i: (0, 0, 0)),
                  pl.BlockSpec((1, BRANCH_W), lambda bb, i: (0, 0))],
        out_specs=pl.BlockSpec((None, tt, BRANCH_W), lambda bb, i: (bb, i, 0)),
        out_shape=jax.ShapeDtypeStruct((b, ta, BRANCH_W), F32),
        compiler_params=_params(("arbitrary",) * 2),
        name="pool",
    )(z, z, z, pool_w, pool_scale)


def _shifted(z, zprev, znext, mu, first, last):
    tt = z.shape[0]
    ridx = lax.broadcasted_iota(jnp.int32, (tt, 1), 0)
    zp = pltpu.roll(z, 1, 0)
    zp = jnp.where(ridx == 0, zprev[7:8, :], zp)
    zp = jnp.where(first, 0.0, zp)
    zn = pltpu.roll(z, tt - 1, 0)
    zn = jnp.where(ridx == tt - 1, znext[0:1, :], zn)
    zn = jnp.where(last, 0.0, zn)
    return z + mu[0:1, :] * (zp - z) + mu[1:2, :] * (zn - z)


def _prep_kernel(rp_ref, rc_ref, rn_ref, lp_ref, lc_ref, ln_ref, mur_ref, mul_ref, w0_ref, w2_ref, a0_ref, a2_ref,
                 g2_ref, kkw_ref, kaw_ref, brk_ref, bd_ref,
                 r_ref, v_ref, kk_ref, lw_ref, kd_ref, ka_ref, bon_ref, gate_ref, *, tt, ctx_len, ta):
    i = pl.program_id(1)
    grow = i * tt + lax.broadcasted_iota(jnp.int32, (tt, 1), 0)
    first = (grow == 0) | (grow == ctx_len)
    last = (grow == ctx_len - 1) | (grow == ta - 1)
    rkv = _shifted(rc_ref[...], rp_ref[...], rn_ref[...], mur_ref[...], first, last)
    lo = _shifted(lc_ref[...], lp_ref[...], ln_ref[...], mul_ref[...], first, last)
    c = BRANCH_W
    r, k, v = rkv[:, :c], rkv[:, c:2 * c], rkv[:, 2 * c:]
    y = _bdot(jnp.tanh(lo), w2_ref[...]) + w0_ref[...]
    lw = -jnp.exp(-0.5) * jax.nn.sigmoid(y)
    a = jax.nn.sigmoid(_bdot(lo, a2_ref[...]) + a0_ref[...])
    gate_ref[...] = _bdot(jax.nn.sigmoid(lo), g2_ref[...])
    bd = bd_ref[...]
    kk = k * kkw_ref[...]
    ss = _dot2(kk * kk, bd)
    kk = kk / jnp.maximum(jnp.sqrt(ss), 1e-12)
    a_f, a_b = a[:, :c], a[:, c:]
    kaw = kaw_ref[...]
    kd_f = k * (1.0 + (a_f - 1.0) * kaw)
    kd_b = k * (1.0 + (a_b - 1.0) * kaw)
    r_ref[...] = r
    v_ref[...] = v.astype(BF16)
    kk_ref[...] = kk
    lw_ref[...] = lw
    kd_ref[:, :c] = kd_f
    kd_ref[:, c:] = kd_b
    ka_ref[:, :c] = a_f * kk
    ka_ref[:, c:] = a_b * kk
    bon_ref[...] = _dot2(r * (kd_f + kd_b) * brk_ref[...], bd) * v


def _prep(z, mur, mul, w0, w2f, a0, a2f, g2f, kkw, kaw, brk, bd, tt, ctx_len):
    b, ta, _ = z.shape
    nb = ta // 8
    c = BRANCH_W
    rb, lb = COL_RKV // (3 * c), COL_LORA // LORA_W
    prev = lambda bb, i: (bb, jnp.maximum(i * (tt // 8) - 1, 0))
    nxt = lambda bb, i: (bb, jnp.minimum((i + 1) * (tt // 8), nb - 1))
    full = lambda shape: pl.BlockSpec(shape, lambda bb, i: (0,) * len(shape))
    tok = lambda w: pl.BlockSpec((None, tt, w), lambda bb, i: (bb, i, 0))
    sds = lambda w: jax.ShapeDtypeStruct((b, ta, w), F32)
    return pl.pallas_call(
        functools.partial(_prep_kernel, tt=tt, ctx_len=ctx_len, ta=ta),
        grid=(b, ta // tt),
        in_specs=[pl.BlockSpec((None, 8, 3 * c), lambda bb, i: prev(bb, i) + (rb,)),
                  pl.BlockSpec((None, tt, 3 * c), lambda bb, i: (bb, i, rb)),
                  pl.BlockSpec((None, 8, 3 * c), lambda bb, i: nxt(bb, i) + (rb,)),
                  pl.BlockSpec((None, 8, LORA_W), lambda bb, i: prev(bb, i) + (lb,)),
                  pl.BlockSpec((None, tt, LORA_W), lambda bb, i: (bb, i, lb)),
                  pl.BlockSpec((None, 8, LORA_W), lambda bb, i: nxt(bb, i) + (lb,)),
                  full((2, 3 * c)), full((2, LORA_W)), full((1, 2 * c)), full((LORA_W, 2 * c)), full((1, 2 * c)),
                  full((LORA_W, 2 * c)), full((LORA_W, c)), full((1, c)), full((1, c)), full((1, c)), full((c, c))],
        out_specs=[tok(c), tok(c), tok(c), tok(2 * c), tok(2 * c), tok(2 * c), tok(c), tok(c)],
        out_shape=[sds(c), jax.ShapeDtypeStruct((b, ta, c), BF16), sds(c), sds(2 * c), sds(2 * c), sds(2 * c),
                   sds(c), sds(c)],
        compiler_params=_params(("arbitrary",) * 2, 40 * 1024 * 1024),
        name="rwkv_prep",
    )(z, z, z, z, z, z, mur, mul, w0, w2f, a0, a2f, g2f, kkw, kaw, brk, bd)


def _wkv_pre_kernel(r_ref, v_ref, kk_ref, lw_ref, kd_ref, ka_ref,
                    kqt_ref, rqt_ref, u0_ref, o0_ref, xt_ref, pt_ref, *, g_chunks):
    d = pl.program_id(1)
    C = CHUNK
    n = g_chunks * C
    sgn = 1 - 2 * d
    row = lax.broadcasted_iota(jnp.int32, (n, n), 0)
    col = lax.broadcasted_iota(jnp.int32, (n, n), 1)
    same = (row // C) == (col // C)
    tri = jnp.where(same, jnp.where((col - row) * sgn <= 0, 1.0, 0.0), 0.0).astype(BF16)
    lw = lw_ref[...]
    cs = _ldot3(tri, lw)
    kd = kd_ref[...]
    ka = ka_ref[...]
    kq_all = kk_ref[...] * jnp.exp(cs - lw)
    rq_all = r_ref[...] * jnp.exp(cs)
    p_inv = jnp.exp(-cs)
    kdi_all = kd * p_inv
    kai_all = ka * p_inv
    v_all = v_ref[...]

    t_i = lax.broadcasted_iota(jnp.int32, (C, 2 * C), 0)
    s_i = lax.broadcasted_iota(jnp.int32, (C, 2 * C), 1) & (C - 1)
    strict = (s_i - t_i) * sgn < 0
    incl = (s_i - t_i) * sgn <= 0
    lane = lax.broadcasted_iota(jnp.int32, (1, 2 * HD), 1)
    m1 = jnp.where(lane < HD, 1.0, 0.0)
    m2 = 1.0 - m1
    r2 = lax.broadcasted_iota(jnp.int32, (2 * HD, 2 * HD), 0)
    c2 = lax.broadcasted_iota(jnp.int32, (2 * HD, 2 * HD), 1)
    eye = jnp.where(r2 == c2, 1.0, 0.0)
    stack2 = lambda x: jnp.concatenate([x * m1, x * m2], axis=0)

    for gc in range(g_chunks):
        rows = slice(gc * C, (gc + 1) * C)
        cs_c = cs[rows]
        tot = jnp.where(d == 1, cs_c[0:1, :], cs_c[C - 1:C, :])
        p_end = jnp.exp(tot - cs_c)
        pt_ref[gc] = jnp.broadcast_to(jnp.exp(tot), (8, tot.shape[-1]))
        kde = kd[rows] * p_end
        kae = ka[rows] * p_end
        for p in range(HEADS // 2):
            sl = slice(2 * HD * p, 2 * HD * (p + 1))
            xt_ref[gc, p] = jnp.concatenate([kde[:, sl], -kae[:, sl]], axis=0).T.astype(BF16)

    units = [(slice(gc * C, (gc + 1) * C), slice(2 * HD * p, 2 * HD * (p + 1)))
             for gc in range(g_chunks) for p in range(HEADS // 2)]
    each = lambda f, *lists: [f(*a) for a in zip(*lists)]
    kq = [kq_all[rows, sl] for rows, sl in units]
    rq = [rq_all[rows, sl] for rows, sl in units]
    g = [_bdot_nt(jnp.concatenate([kq[n_], rq[n_]], axis=0),
                  jnp.concatenate([stack2(kdi_all[rows, sl]), stack2(kai_all[rows, sl])], axis=0))
         for n_, (rows, sl) in enumerate(units)]
    akk = [jnp.where(strict, x[:C, :2 * C], 0.0) for x in g]
    ark = [jnp.where(incl, x[C:, :2 * C], 0.0) for x in g]
    ara = [jnp.where(incl, x[C:, 2 * C:], 0.0) for x in g]
    lmat = [stack2(jnp.where(strict, x[:C, 2 * C:], 0.0)) for x in g]
    tm = [eye - x for x in lmat]
    pw = each(_bdot, lmat, lmat)
    n_sq = CHUNK.bit_length() - 2
    for it in range(n_sq):
        tm = each(lambda t_, p_: t_ + _bdot(t_, p_), tm, pw)
        if it + 1 < n_sq:
            pw = each(_bdot, pw, pw)
    tss = [x[:C, :] + x[C:, :] for x in tm]
    vst = [stack2(v_all[rows, sl].astype(F32)) for rows, sl in units]
    kqt = each(lambda t_, k_: _bdot(t_, stack2(k_)), tss, kq)
    akv = each(_bdot, akk, vst)
    u0 = each(lambda t_, a_: _bdot(t_, stack2(a_)), tss, akv)
    rqt = each(lambda r_, a_, k_: r_ - _bdot(a_, stack2(k_)), rq, ara, kqt)
    o0 = each(lambda ak_, aa_, v_, u_: _bdot(jnp.concatenate([ak_, -aa_], axis=1),
                                             jnp.concatenate([v_, stack2(u_)], axis=0)), ark, ara, vst, u0)
    for n_, (rows, sl) in enumerate(units):
        kqt_ref[rows, sl] = kqt[n_].astype(BF16)
        rqt_ref[rows, sl] = rqt[n_].astype(BF16)
        u0_ref[rows, sl] = u0[n_]
        o0_ref[rows, sl] = o0[n_]


def _wkv_seq_kernel(*refs, n_batch):
    ins, (of_ref, ob_ref), st_ref = refs[:14], refs[14:16], refs[16]
    ci = pl.program_id(0)
    C = CHUNK

    @pl.when(ci == 0)
    def _():
        st_ref[...] = jnp.zeros_like(st_ref)

    r2 = lax.broadcasted_iota(jnp.int32, (2 * HD, 2 * HD), 0)
    c2 = lax.broadcasted_iota(jnp.int32, (2 * HD, 2 * HD), 1)
    bdm = (r2 < HD) == (c2 < HD)
    for d in range(2):
        kqt_ref, rqt_ref, u0_ref, o0_ref, xt_ref, pt_ref, v_ref = ins[7 * d:7 * d + 7]
        o_ref = of_ref if d == 0 else ob_ref
        for bb in range(n_batch):
            for p in range(HEADS // 2):
                sl = slice(2 * HD * p, 2 * HD * (p + 1))
                st = st_ref[d, bb, p]
                lhs = jnp.concatenate([kqt_ref[bb, :, sl], rqt_ref[bb, :, sl]], axis=0)
                uo = jnp.dot(lhs, st.astype(BF16), preferred_element_type=F32)
                u = uo[:C, :] + u0_ref[bb, :, sl]
                o_ref[bb, :, sl] = uo[C:, :] + o0_ref[bb, :, sl]
                yv = jnp.concatenate([v_ref[bb, :, sl], u.astype(BF16)], axis=0)
                upd = jnp.dot(xt_ref[bb, p], yv, preferred_element_type=F32)
                pcol = jnp.broadcast_to(pt_ref[bb, 0:1, sl], (2 * HD, 2 * HD)).T
                st_ref[d, bb, p] = st * pcol + jnp.where(bdm, upd, 0.0)


def _wkv(r, v, kk, lw, kd, ka, ctx_len):
    b, ta, c = r.shape
    nch = ta // CHUNK
    ncc = ctx_len // CHUNK
    gch = 4
    assert nch % gch == 0
    n = gch * CHUNK
    npair = HEADS // 2
    shared = pl.BlockSpec((None, n, c), lambda bb, d, i: (bb, i, 0))
    per_dir = pl.BlockSpec((None, n, c), lambda bb, d, i: (bb, i, d))
    tok = pl.BlockSpec((None, None, n, c), lambda bb, d, i: (d, bb, i, 0))
    kqt, rqt, u0, o0, xt, pt = pl.pallas_call(
        functools.partial(_wkv_pre_kernel, g_chunks=gch),
        grid=(b, 2, nch // gch),
        in_specs=[shared, shared, shared, per_dir, per_dir, per_dir],
        out_specs=[tok, tok, tok, tok,
                   pl.BlockSpec((None, None, gch, npair, 2 * HD, 2 * CHUNK), lambda bb, d, i: (d, bb, i, 0, 0, 0)),
                   pl.BlockSpec((None, None, gch, 8, c), lambda bb, d, i: (d, bb, i, 0, 0))],
        out_shape=[jax.ShapeDtypeStruct((2, b, ta, c), BF16), jax.ShapeDtypeStruct((2, b, ta, c), BF16),
                   jax.ShapeDtypeStruct((2, b, ta, c), F32), jax.ShapeDtypeStruct((2, b, ta, c), F32),
                   jax.ShapeDtypeStruct((2, b, nch, npair, 2 * HD, 2 * CHUNK), BF16),
                   jax.ShapeDtypeStruct((2, b, nch, 8, c), F32)],
        compiler_params=_params(("arbitrary",) * 3, 40 * 1024 * 1024),
        name="wkv_pre",
    )(r, v, kk, lw, kd, ka)

    def chunk(d, ci):
        return ci if d == 0 else jnp.where(ci < ncc, ncc - 1 - ci, nch + ncc - 1 - ci)

    in_specs, args = [], []
    for d in range(2):
        tokd = pl.BlockSpec((None, b, CHUNK, c), lambda ci, d=d: (d, 0, chunk(d, ci), 0))
        in_specs += [tokd, tokd, tokd, tokd,
                     pl.BlockSpec((None, b, None, npair, 2 * HD, 2 * CHUNK),
                                  lambda ci, d=d: (d, 0, chunk(d, ci), 0, 0, 0)),
                     pl.BlockSpec((None, b, None, 8, c), lambda ci, d=d: (d, 0, chunk(d, ci), 0, 0)),
                     pl.BlockSpec((b, CHUNK, c), lambda ci, d=d: (0, chunk(d, ci), 0))]
        args += [kqt, rqt, u0, o0, xt, pt, v]
    return pl.pallas_call(
        functools.partial(_wkv_seq_kernel, n_batch=b),
        grid=(nch,),
        in_specs=in_specs,
        out_specs=[pl.BlockSpec((b, CHUNK, c), lambda ci, d=d: (0, chunk(d, ci), 0)) for d in range(2)],
        out_shape=[jax.ShapeDtypeStruct((b, ta, c), F32)] * 2,
        scratch_shapes=[pltpu.VMEM((2, b, npair, 2 * HD, 2 * HD), F32)],
        compiler_params=_params(("arbitrary",)),
        name="wkv_seq",
    )(*args)


def _qknorm_kernel(q_ref, k_ref, v_ref, gq_ref, gk_ref, bd_ref, qo_ref, ko_ref, vo_ref):
    bd = bd_ref[...]

    def norm(x, g):
        ms = _dot2(x * x, bd) * (1.0 / HD)
        return x * lax.rsqrt(ms + RMS_EPS) * g

    qo_ref[...] = (norm(q_ref[...], gq_ref[...]) * NAT_SCALE).astype(BF16)
    ko_ref[...] = norm(k_ref[...], gk_ref[...]).astype(BF16)
    vo_ref[...] = v_ref[...].astype(BF16)


def _qknorm(z, gq, gk, bd, tt):
    b, ta, _ = z.shape
    c = BRANCH_W
    col = lambda cb: pl.BlockSpec((None, tt, c), lambda bb, i: (bb, i, cb))
    full = lambda shape: pl.BlockSpec(shape, lambda bb, i: (0,) * len(shape))
    out = pl.BlockSpec((None, tt, c), lambda bb, i: (bb, i, 0))
    sds = jax.ShapeDtypeStruct((b, ta, c), BF16)
    return pl.pallas_call(
        _qknorm_kernel,
        grid=(b, ta // tt),
        in_specs=[col(COL_Q // c), col(COL_K // c), col(COL_V // c), full((1, c)), full((1, c)), full((c, c))],
        out_specs=[out, out, out],
        out_shape=[sds, sds, sds],
        compiler_params=_params(("arbitrary",) * 2),
        name="nat_qknorm",
    )(z, z, z, gq, gk, bd)


def _nat_kernel(q_ref, kp_ref, kc_ref, kn_ref, vp_ref, vc_ref, vn_ref, kx_ref, vx_ref, bias_ref, o_ref,
                kbuf, vbuf, *, tt):
    i = pl.program_id(1)
    lane = lax.broadcasted_iota(jnp.int32, (1, 2 * HD), 1)
    first = lane < HD

    def softmax_pv(parts):
        mx = None
        for s, _ in parts:
            m = jnp.max(s, axis=-1, keepdims=True)
            mx = m if mx is None else jnp.maximum(mx, m)
        den = 0.0
        acc = 0.0
        for s, vv in parts:
            e = jnp.exp(s - mx)
            den = den + jnp.sum(e, axis=-1, keepdims=True)
            acc = acc + _bdot(e, vv)
        return acc / den

    @pl.when(i == 0)
    def _():
        for p in range(HEADS // 2):
            sl = slice(2 * HD * p, 2 * HD * (p + 1))
            qp = q_ref[:, sl]
            kx = kx_ref[:, sl]
            vx = vx_ref[:, sl]
            outs = []
            for h in range(2):
                qm = jnp.where(first if h == 0 else ~first, qp, jnp.zeros_like(qp))
                outs.append(softmax_pv([(_bdot_nt(qm, kx), vx)]))
            o_ref[:, sl] = jnp.where(first, outs[0], outs[1])

    @pl.when(i > 0)
    def _():
        for n_, (kr, vr) in enumerate(((kp_ref, vp_ref), (kc_ref, vc_ref), (kn_ref, vn_ref), (kx_ref, vx_ref))):
            kbuf[n_ * tt:(n_ + 1) * tt, :] = kr[...]
            vbuf[n_ * tt:(n_ + 1) * tt, :] = vr[...]

        def scores(hd):
            sl = slice(2 * HD * (hd // 2), 2 * HD * (hd // 2 + 1))
            qp = q_ref[:, sl]
            qm = jnp.where(first if hd % 2 == 0 else ~first, qp, jnp.zeros_like(qp))
            return _bdot_nt(qm, kbuf[:, sl])

        s_next = scores(0)
        outs = []
        for hd in range(HEADS):
            s = s_next
            if hd + 1 < HEADS:
                s_next = scores(hd + 1)
            sl = slice(2 * HD * (hd // 2), 2 * HD * (hd // 2 + 1))
            s_loc = s[:, :3 * tt] + bias_ref[hd]
            s_ctx = s[:, 3 * tt:]
            mx = jnp.maximum(jnp.max(s_loc, axis=-1, keepdims=True), jnp.max(s_ctx, axis=-1, keepdims=True))
            e = jnp.concatenate([jnp.exp(s_loc - mx), jnp.exp(s_ctx - mx)], axis=1)
            den = jnp.sum(e, axis=-1, keepdims=True)
            outs.append(_bdot(e, vbuf[:, sl]) / den)
            if hd % 2 == 1:
                o_ref[:, sl] = jnp.where(first, outs[hd - 1], outs[hd])


def _nat(qn, kn, vb, bias3, tt, ctx_len):
    b, ta, c = qn.shape
    nt = ta // tt
    cur = pl.BlockSpec((None, tt, c), lambda bb, i: (bb, i, 0))
    prv = pl.BlockSpec((None, tt, c), lambda bb, i: (bb, jnp.maximum(i - 1, 0), 0))
    nxt = pl.BlockSpec((None, tt, c), lambda bb, i: (bb, jnp.minimum(i + 1, nt - 1), 0))
    cx = pl.BlockSpec((None, tt, c), lambda bb, i: (bb, 0, 0))
    variant = lambda i: jnp.where(i <= 1, 0, jnp.where(i == nt - 1, 2, 1))
    return pl.pallas_call(
        functools.partial(_nat_kernel, tt=tt),
        grid=(b, nt),
        in_specs=[cur, prv, cur, nxt, prv, cur, nxt, cx, cx,
                  pl.BlockSpec((None, HEADS, tt, 3 * tt), lambda bb, i: (variant(i), 0, 0, 0))],
        out_specs=pl.BlockSpec((None, tt, c), lambda bb, i: (bb, i, 0)),
        out_shape=jax.ShapeDtypeStruct((b, ta, c), F32),
        scratch_shapes=[pltpu.VMEM((4 * tt, c), BF16), pltpu.VMEM((4 * tt, c), BF16)],
        compiler_params=_params(("arbitrary",) * 2, 48 * 1024 * 1024),
        name="nat",
    )(qn, kn, kn, kn, vb, vb, vb, kn, vb, bias3)


def _nat_bias_table(rpb, tt):
    h = rpb.shape[0]
    rpt = tt // GRID_W
    cols = jnp.arange(GRID_W)
    cstart = jnp.clip(cols - NAT_KW // 2, 0, GRID_W - NAT_KW)
    kc = cols[None, :]
    col_ok = (kc >= cstart[:, None]) & (kc < cstart[:, None] + NAT_KW)
    idx = jnp.clip(kc - cols[:, None] + (NAT_KW - 1), 0, 2 * NAT_KW - 2)
    t15 = jnp.where(col_ok[None, None], rpb[:, :, idx], NEG).astype(F32)
    a = jnp.arange(rpt)[:, None]
    j = jnp.arange(3 * rpt)[None, :]
    d = j - a - rpt + (NAT_KH - 1)
    blocks = t15[:, d]
    lo = jnp.stack([jnp.full((rpt,), rpt), a[:, 0] + rpt - NAT_KH // 2, jnp.full((rpt,), 2 * rpt - NAT_KH)])
    row_ok = (j[None] >= lo[:, :, None]) & (j[None] < lo[:, :, None] + NAT_KH)
    full = jnp.where(row_ok[:, None, :, :, None, None], blocks[None], NEG)
    return full.transpose(0, 1, 2, 4, 3, 5).reshape(3, h, tt, 3 * tt)


def _merge_kernel(x_ref, pool_ref, of_ref, ob_ref, bon_ref, gate_ref, nat_ref, zg_ref, modc_ref, modb_ref,
                  gng_ref, gnb_ref, bd_ref, wb_ref, wo_ref, o_ref, *, tt, ctx_len):
    i = pl.program_id(1)
    bd = bd_ref[...]
    wkv = of_ref[...] + ob_ref[...]
    mu = _dot2(wkv, bd) * (1.0 / HD)
    cen = wkv - mu
    var = _dot2(cen * cen, bd) * (1.0 / HD)
    y = cen * lax.rsqrt(var + GN_EPS) * gng_ref[...] + gnb_ref[...]
    rw = (y + bon_ref[...]) * gate_ref[...]
    d = x_ref.shape[-1]
    m = jnp.zeros((tt, d), F32)
    for n, br in enumerate((pool_ref[...], rw, nat_ref[...])):
        m = m + jax.nn.sigmoid(zg_ref[:, n * d:(n + 1) * d]) * _bdot(br, wb_ref[n])
    yout = _bdot(m, wo_ref[...])
    o_ref[...] = x_ref[...] + _gate_rows(modc_ref, modb_ref, i * tt, tt, ctx_len, 2) * yout


def _merge(xa, pool, o2, bonus, gate, nat, z, modc, modb, gng, gnb, bd, wb, wo, tt, ctx_len):
    b, ta, d = xa.shape
    c = BRANCH_W
    tok = lambda w: pl.BlockSpec((None, tt, w), lambda bb, i: (bb, i, 0))
    full = lambda shape: pl.BlockSpec(shape, lambda bb, i: (0,) * len(shape))
    return pl.pallas_call(
        functools.partial(_merge_kernel, tt=tt, ctx_len=ctx_len),
        grid=(b, ta // tt),
        in_specs=[tok(d), tok(c), tok(c), tok(c), tok(c), tok(c), tok(c),
                  pl.BlockSpec((None, tt, ZG_W), lambda bb, i: (bb, i, COL_G // ZG_W)),
                  full((6, d)), pl.BlockSpec((None, 6, d), lambda bb, i: (bb, 0, 0)),
                  full((1, c)), full((1, c)), full((c, c)), full((3, c, d)), full((d, d))],
        out_specs=tok(d),
        out_shape=jax.ShapeDtypeStruct((b, ta, d), F32),
        compiler_params=_params(("arbitrary",) * 2, 48 * 1024 * 1024),
        name="merge",
    )(xa, pool, o2[0], o2[1], bonus, gate, nat, z, modc, modb, gng, gnb, bd, wb, wo)


def _ffn_kernel(x_ref, g_ref, modc_ref, modb_ref, w1_ref, w3_ref, w2_ref, o_ref, h_scr, acc, *, tm, ctx_len):
    i = pl.program_id(1)
    f = pl.program_id(2)

    @pl.when(f == 0)
    def _():
        h_scr[...] = _norm_mod(x_ref[...], g_ref[...], modc_ref, modb_ref, i * tm, ctx_len, 3, 4).astype(BF16)
        acc[...] = jnp.zeros_like(acc)

    h = h_scr[...]
    a = jnp.dot(h, w1_ref[...], preferred_element_type=F32)
    bb = jnp.dot(h, w3_ref[...], preferred_element_type=F32)
    acc[...] += _bdot(_silu(a) * bb, w2_ref[...])

    @pl.when(f == pl.num_programs(2) - 1)
    def _():
        o_ref[...] = x_ref[...] + _gate_rows(modc_ref, modb_ref, i * tm, tm, ctx_len, 5) * acc[...]


def _ffn(xa, g, modc, modb, w1, w3, w2, tm, fc, ctx_len):
    b, ta, d = xa.shape
    dff = w1.shape[1]
    tok = pl.BlockSpec((None, tm, d), lambda bb, i, f: (bb, i, 0))
    return pl.pallas_call(
        functools.partial(_ffn_kernel, tm=tm, ctx_len=ctx_len),
        grid=(b, ta // tm, dff // fc),
        in_specs=[tok,
                  pl.BlockSpec((1, d), lambda bb, i, f: (0, 0)),
                  pl.BlockSpec((6, d), lambda bb, i, f: (0, 0)),
                  pl.BlockSpec((None, 6, d), lambda bb, i, f: (bb, 0, 0)),
                  pl.BlockSpec((d, fc), lambda bb, i, f: (0, f)),
                  pl.BlockSpec((d, fc), lambda bb, i, f: (0, f)),
                  pl.BlockSpec((fc, d), lambda bb, i, f: (f, 0))],
        out_specs=tok,
        out_shape=jax.ShapeDtypeStruct((b, ta, d), F32),
        scratch_shapes=[pltpu.VMEM((tm, d), BF16), pltpu.VMEM((tm, d), F32)],
        compiler_params=_params(("arbitrary",) * 3, 48 * 1024 * 1024),
        name="ffn",
    )(xa, g, modc, modb, w1, w3, w2)


def _route_kernel(x_ref, g_ref, modc_ref, modb_ref, rt_ref, h_ref, r_ref, *, tm, ctx_len, n_exp):
    i = pl.program_id(1)
    lane = lax.broadcasted_iota(jnp.int32, (tm, 128), 1)
    h = _norm_mod(x_ref[...], g_ref[...], modc_ref, modb_ref, i * tm, ctx_len, 3, 4)
    h_ref[...] = h
    logits = jnp.dot(h, rt_ref[...], precision=lax.Precision.HIGHEST, preferred_element_type=F32)
    logits = jnp.where(lane < n_exp, logits, NEG)
    m1 = jnp.max(logits, axis=-1, keepdims=True)
    i1 = jnp.min(jnp.where(logits == m1, lane, 128), axis=-1, keepdims=True)
    rest = jnp.where(lane == i1, NEG, logits)
    m2 = jnp.max(rest, axis=-1, keepdims=True)
    i2 = jnp.min(jnp.where(rest == m2, lane, 128), axis=-1, keepdims=True)
    e2 = jnp.exp(m2 - m1)
    den = 1.0 + e2
    r_ref[...] = jnp.where(lane == 0, i1.astype(F32),
                           jnp.where(lane == 1, i2.astype(F32), jnp.where(lane == 2, 1.0 / den, e2 / den)))


def _route(xa, g, modc, modb, router, tm, ctx_len, n_exp):
    b, ta, d = xa.shape
    tok = lambda w: pl.BlockSpec((None, tm, w), lambda bb, i: (bb, i, 0))
    return pl.pallas_call(
        functools.partial(_route_kernel, tm=tm, ctx_len=ctx_len, n_exp=n_exp),
        grid=(b, ta // tm),
        in_specs=[tok(d),
                  pl.BlockSpec((1, d), lambda bb, i: (0, 0)),
                  pl.BlockSpec((6, d), lambda bb, i: (0, 0)),
                  pl.BlockSpec((None, 6, d), lambda bb, i: (bb, 0, 0)),
                  pl.BlockSpec((d, 128), lambda bb, i: (0, 0))],
        out_specs=[tok(d), tok(128)],
        out_shape=[jax.ShapeDtypeStruct((b, ta, d), F32), jax.ShapeDtypeStruct((b, ta, 128), F32)],
        compiler_params=_params(("arbitrary",) * 2, 40 * 1024 * 1024),
        name="moe_route",
    )(xa, g, modc, modb, router)


def _experts_kernel(te_ref, nact_ref, src_ref, dst_ref, h_hbm, wrow_ref, w1_ref, w3_ref, w2_ref, y_hbm,
                    hbuf, hsel, acc, gsem, ssem, *, ts):
    j = pl.program_id(0)
    f = pl.program_id(1)
    nj = pl.num_programs(0)
    nf = pl.num_programs(1)
    slot = j % 2

    def gather_copy(tile, sl_, r):
        return pltpu.make_async_copy(h_hbm.at[pl.ds(src_ref[tile * ts + r], 1)], hbuf.at[sl_, pl.ds(r, 1)],
                                     gsem.at[sl_])

    def scatter_copy(tile, r):
        return pltpu.make_async_copy(acc.at[pl.ds(r, 1)], y_hbm.at[pl.ds(dst_ref[tile * ts + r], 1)], ssem.at[0])

    def for_rows(fn):
        def body(r, carry):
            fn(r)
            return carry
        lax.fori_loop(0, ts, body, 0, unroll=8)

    @pl.when(f == 0)
    def _():
        @pl.when(j == 0)
        def _():
            for_rows(lambda r: gather_copy(j, slot, r).start())

        for_rows(lambda r: gather_copy(j, slot, r).wait())
        hsel[...] = hbuf[slot].astype(BF16)

        @pl.when(j > 0)
        def _():
            for_rows(lambda r: scatter_copy(j - 1, r).wait())

        acc[...] = jnp.zeros_like(acc)

    @pl.when((f == nf - 1) & (j + 1 < nj))
    def _():
        for_rows(lambda r: gather_copy(j + 1, 1 - slot, r).start())

    @pl.when(j < nact_ref[0])
    def _():
        h = hsel[...]
        a = jnp.dot(h, w1_ref[...], preferred_element_type=F32)
        bb = jnp.dot(h, w3_ref[...], preferred_element_type=F32)
        acc[...] += _bdot(_silu(a) * bb * wrow_ref[...], w2_ref[...])

    @pl.when(f == nf - 1)
    def _():
        for_rows(lambda r: scatter_copy(j, r).start())

        @pl.when(j == nj - 1)
        def _():
            for_rows(lambda r: scatter_copy(j, r).wait())


def _experts(h2, te, nact, src, dst, wrow, w1, w3, w2, ts, fc, n_out_rows):
    n, d = h2.shape
    n_exp, _, dff = w1.shape
    nt = src.shape[0] // ts
    return pl.pallas_call(
        functools.partial(_experts_kernel, ts=ts),
        grid_spec=pltpu.PrefetchScalarGridSpec(
            num_scalar_prefetch=4,
            grid=(nt, dff // fc),
            in_specs=[pl.BlockSpec(memory_space=pl.ANY),
                      pl.BlockSpec((ts, 1), lambda j, f, te_, na_, s_, d_: (j, 0)),
                      pl.BlockSpec((None, d, fc), lambda j, f, te_, na_, s_, d_: (te_[j], 0, f)),
                      pl.BlockSpec((None, d, fc), lambda j, f, te_, na_, s_, d_: (te_[j], 0, f)),
                      pl.BlockSpec((None, fc, d), lambda j, f, te_, na_, s_, d_: (te_[j], f, 0))],
            out_specs=pl.BlockSpec(memory_space=pl.ANY),
            scratch_shapes=[pltpu.VMEM((2, ts, d), F32), pltpu.VMEM((ts, d), BF16), pltpu.VMEM((ts, d), F32),
                            pltpu.SemaphoreType.DMA((2,)), pltpu.SemaphoreType.DMA((1,))]),
        out_shape=jax.ShapeDtypeStruct((n_out_rows, d), F32),
        compiler_params=_params(("arbitrary",) * 2, 52 * 1024 * 1024),
        name="moe_experts",
    )(te, nact, src, dst, h2, wrow, w1, w3, w2)


def _combine_kernel(x_ref, y0_ref, y1_ref, modc_ref, modb_ref, o_ref, *, tt, ctx_len):
    i = pl.program_id(1)
    o_ref[...] = x_ref[...] + _gate_rows(modc_ref, modb_ref, i * tt, tt, ctx_len, 5) * (y0_ref[...] + y1_ref[...])


def _combine(xa, y, modc, modb, tt, ctx_len):
    b, ta, d = xa.shape
    nt = ta // tt
    tok = pl.BlockSpec((None, tt, d), lambda bb, i: (bb, i, 0))
    return pl.pallas_call(
        functools.partial(_combine_kernel, tt=tt, ctx_len=ctx_len),
        grid=(b, nt),
        in_specs=[tok,
                  pl.BlockSpec((tt, d), lambda bb, i: (bb * nt + i, 0)),
                  pl.BlockSpec((tt, d), lambda bb, i: (b * nt + bb * nt + i, 0)),
                  pl.BlockSpec((6, d), lambda bb, i: (0, 0)),
                  pl.BlockSpec((None, 6, d), lambda bb, i: (bb, 0, 0))],
        out_specs=tok,
        out_shape=jax.ShapeDtypeStruct((b, ta, d), F32),
        compiler_params=_params(("arbitrary",) * 2),
        name="moe_combine",
    )(xa, y, y, modc, modb)


def _moe(xa, g, modc, modb, router, w1, w3, w2, tm, ctx_len):
    b, ta, d = xa.shape
    n_exp = router.shape[-1]
    n = b * ta
    ts = MOE_TILE
    rt = jnp.zeros((d, 128), F32).at[:, :n_exp].set(router)
    h, route = _route(xa, g, modc, modb, rt, tm, ctx_len, n_exp)
    route = route.reshape(n, 128)
    e_s = route[:, :TOP_K].astype(jnp.int32).reshape(-1)
    w_s = route[:, TOP_K:2 * TOP_K].reshape(-1)
    onehot = (e_s[:, None] == jnp.arange(n_exp)[None, :]).astype(jnp.int32)
    cum = jnp.cumsum(onehot, axis=0)
    rank = jnp.take_along_axis(cum, e_s[:, None], axis=1)[:, 0] - 1
    padded = ((cum[-1] + ts - 1) // ts) * ts
    ends = jnp.cumsum(padded)
    pos = (ends - padded)[e_s] + rank
    p_rows = TOP_K * n + n_exp * ts
    s_idx = jnp.arange(TOP_K * n, dtype=jnp.int32)
    src = jnp.zeros((p_rows,), jnp.int32).at[pos].set(s_idx // TOP_K)
    is_pad = jnp.ones((p_rows,), jnp.int32).at[pos].set(0)
    dst = (TOP_K * n - 1 + jnp.cumsum(is_pad)).astype(jnp.int32).at[pos].set((s_idx % TOP_K) * n + s_idx // TOP_K)
    wrow = jnp.zeros((p_rows,), F32).at[pos].set(w_s).reshape(p_rows, 1)
    tile_start = jnp.arange(p_rows // ts, dtype=jnp.int32) * ts
    te = jnp.minimum(jnp.sum(tile_start[:, None] >= ends[None, :], axis=1), n_exp - 1).astype(jnp.int32)
    nact = (ends[-1:] // ts).astype(jnp.int32)
    fc = w1.shape[-1] // 2
    y = _experts(h.reshape(n, d), te, nact, src, dst, wrow, w1, w3, w2, ts, fc, p_rows)
    return _combine(xa, y, modc, modb, TOK_TILE, ctx_len)


def _layer_weights(li, w_in, pool_w, pool_scale, shift_mu, decay_w0, decay_w2, iclr_a0, iclr_a2, key_kk, key_ka,
                   bonus_rk, gn_g, gn_b, gate_g2, nat_qn_g, nat_kn_g, nat_rpb, w_branch, w_out):
    c = BRANCH_W
    d = w_in.shape[1]
    wi = w_in[li]
    o_r = c
    o_lora = o_r + 3 * c
    n_lora = 2 * DECAY_LORA + 2 * ICLR_LORA + GATE_LORA
    o_q = o_lora + n_lora
    o_g = o_q + 3 * c
    pad = jnp.zeros((d, LORA_W - n_lora), F32)
    w_z = jnp.concatenate([wi[:, o_g:], wi[:, o_r:o_lora], wi[:, o_q:o_g], wi[:, :c], wi[:, o_lora:o_q], pad],
                          axis=1).astype(BF16)
    mu = shift_mu[li]
    mur = mu[:, :3 * c]
    mul = jnp.concatenate([mu[:, 3 * c:], jnp.zeros((2, LORA_W - n_lora), F32)], axis=1)
    w2f = jnp.zeros((LORA_W, 2 * c), F32)
    a2f = jnp.zeros((LORA_W, 2 * c), F32)
    for dd in range(2):
        w2f = w2f.at[dd * DECAY_LORA:(dd + 1) * DECAY_LORA, dd * c:(dd + 1) * c].set(decay_w2[li, dd])
        a2f = a2f.at[2 * DECAY_LORA + dd * ICLR_LORA:2 * DECAY_LORA + (dd + 1) * ICLR_LORA,
                     dd * c:(dd + 1) * c].set(iclr_a2[li, dd])
    o_gl = 2 * DECAY_LORA + 2 * ICLR_LORA
    g2f = jnp.zeros((LORA_W, c), F32).at[o_gl:o_gl + GATE_LORA].set(gate_g2[li])
    return dict(
        w_z=w_z, mur=mur, mul=mul,
        w0=decay_w0[li].reshape(1, 2 * c), w2f=w2f.astype(BF16),
        a0=iclr_a0[li].reshape(1, 2 * c), a2f=a2f.astype(BF16), g2f=g2f.astype(BF16),
        kkw=key_kk[li].reshape(1, c), kaw=key_ka[li].reshape(1, c), brk=bonus_rk[li].reshape(1, c),
        gng=gn_g[li].reshape(1, c), gnb=gn_b[li].reshape(1, c),
        gq=jnp.tile(nat_qn_g[li], HEADS).reshape(1, c), gk=jnp.tile(nat_kn_g[li], HEADS).reshape(1, c),
        tab=_nat_bias_table(nat_rpb[li], TOK_TILE),
        pool_w=pool_w[li].astype(BF16), pool_scale=pool_scale[li].reshape(1, c),
        wb=w_branch[li].astype(BF16), wo=w_out[li].astype(BF16),
    )


def kernel(x, c, ctx, c_ctx, w_mod, b_mod, norm_mix_g, norm_ffn_g, w_in, pool_w, pool_scale, shift_mu, decay_w0, decay_w2, iclr_a0, iclr_a2, key_kk, key_ka, bonus_rk, gn_g, gn_b, gate_g2, nat_qn_g, nat_kn_g, nat_rpb, w_branch, w_out, ffn_w1, ffn_w3, ffn_w2, router, moe_w1, moe_w3, moe_w2):
    b, t, d = x.shape
    ctx_len = ctx.shape[1]
    depth = w_mod.shape[0]
    ta = ctx_len + t
    tt = TOK_TILE
    assert ctx_len == tt and t % tt == 0 and t // GRID_W >= NAT_KH and b + 1 <= 8 and CHUNK == HD
    tm = ta // 8
    assert ta % 8 == 0 and tm % 16 == 0
    fc = 256

    xa = jnp.concatenate([ctx, x], axis=1)
    cc = jnp.zeros((8, d), F32).at[:b].set(c).at[b].set(c_ctx)
    mod_all = _modulation(cc, w_mod, b_mod)
    head_id = jnp.arange(BRANCH_W) // HD
    bd = (head_id[:, None] == head_id[None, :]).astype(BF16)

    for li in range(depth):
        lw_ = _layer_weights(li, w_in, pool_w, pool_scale, shift_mu, decay_w0, decay_w2, iclr_a0, iclr_a2, key_kk,
                             key_ka, bonus_rk, gn_g, gn_b, gate_g2, nat_qn_g, nat_kn_g, nat_rpb, w_branch, w_out)
        modb = mod_all[li, :b].reshape(b, 6, d)
        modc = mod_all[li, b].reshape(6, d)
        z = _inproj(xa, norm_mix_g[li].reshape(1, d), modc, modb, lw_["w_z"], tt, ctx_len)
        pool = _pool(z, lw_["pool_w"], lw_["pool_scale"], tt, ctx_len)
        r, v, kk, lwd, kd, ka, bonus, gate = _prep(z, lw_["mur"], lw_["mul"], lw_["w0"], lw_["w2f"], lw_["a0"],
                                                   lw_["a2f"], lw_["g2f"], lw_["kkw"], lw_["kaw"], lw_["brk"], bd,
                                                   tt, ctx_len)
        o2 = _wkv(r, v, kk, lwd, kd, ka, ctx_len)
        qn, kn, vb = _qknorm(z, lw_["gq"], lw_["gk"], bd, tt)
        nat = _nat(qn, kn, vb, lw_["tab"], tt, ctx_len)
        xa = _merge(xa, pool, o2, bonus, gate, nat, z, modc, modb, lw_["gng"], lw_["gnb"], bd, lw_["wb"], lw_["wo"],
                    tt, ctx_len)
        j = li // 2
        gf = norm_ffn_g[li].reshape(1, d)
        if li % 2 == 0:
            xa = _ffn(xa, gf, modc, modb, ffn_w1[j].astype(BF16), ffn_w3[j].astype(BF16), ffn_w2[j].astype(BF16),
                      tm, fc, ctx_len)
        else:
            xa = _moe(xa, gf, modc, modb, router[j], moe_w1[j].astype(BF16), moe_w3[j].astype(BF16),
                      moe_w2[j].astype(BF16), tm, ctx_len)
    return xa[:, ctx_len:]
```

```python
import functools

import jax
import jax.numpy as jnp
from jax import lax
from jax.experimental import pallas as pl
from jax.experimental.pallas import tpu as pltpu

F32 = jnp.float32
BF16 = jnp.bfloat16

GRID_W = 64
BRANCH_W = 512
POOL_WINDOWS = (2, 4, 8, 16)
POOL_GW = 128
HEADS = 8
HD = 64
DECAY_LORA = 32
ICLR_LORA = 32
GATE_LORA = 96
GN_EPS = 64e-5
NAT_KH = 8
NAT_KW = 16
NAT_SCALE = HD ** -0.5
TOP_K = 2
RMS_EPS = 1e-6
NEG = -1e30

ZG_W = 3 * 1024
COL_G = 0
COL_RKV = 3072
COL_Q = 4608
COL_K = 5120
COL_V = 5632
COL_P = 6144
COL_LORA = 6656
Z_W = 6912
LORA_W = 256

TOK_TILE = 256
CHUNK = 64
MOE_TILE = 512
POOL_HALO = 64
V7X_VMEM_LIMIT = 56 * 1024 * 1024


def _bdot(a, b):
    return jnp.dot(a.astype(BF16), b.astype(BF16), preferred_element_type=F32)


def _bdot_nt(a, b):
    return lax.dot_general(a.astype(BF16), b.astype(BF16), (((1,), (1,)), ((), ())), preferred_element_type=F32)


def _split2(x):
    hi = x.astype(BF16)
    lo = (x - hi.astype(F32)).astype(BF16)
    return hi, lo


def _dot2(x, b):
    hi, lo = _split2(x)
    return jnp.dot(hi, b, preferred_element_type=F32) + jnp.dot(lo, b, preferred_element_type=F32)


def _ldot3(a, x):
    h1 = x.astype(BF16)
    r1 = x - h1.astype(F32)
    h2 = r1.astype(BF16)
    h3 = (r1 - h2.astype(F32)).astype(BF16)
    d = lambda h: jnp.dot(a, h, preferred_element_type=F32)
    return d(h1) + d(h2) + d(h3)


def _silu(x):
    return x * jax.nn.sigmoid(x)


def _params(sem, vmem=None, **kw):
    return pltpu.CompilerParams(dimension_semantics=sem, vmem_limit_bytes=vmem, **kw)


def _mod_kernel(c_ref, w_ref, b_ref, o_ref):
    o_ref[...] = jnp.dot(_silu(c_ref[...]), w_ref[...], precision=lax.Precision.HIGHEST,
                         preferred_element_type=F32) + b_ref[...]


def _modulation(cc, w_mod, b_mod):
    depth, d, n = w_mod.shape
    tn = 1536
    return pl.pallas_call(
        _mod_kernel,
        grid=(depth, n // tn),
        in_specs=[pl.BlockSpec((8, d), lambda l, j: (0, 0)),
                  pl.BlockSpec((None, d, tn), lambda l, j: (l, 0, j)),
                  pl.BlockSpec((None, 1, tn), lambda l, j: (l, 0, j))],
        out_specs=pl.BlockSpec((None, 8, tn), lambda l, j: (l, 0, j)),
        out_shape=jax.ShapeDtypeStruct((depth, 8, n), F32),
        compiler_params=_params(("arbitrary", "arbitrary"), 40 * 1024 * 1024),
        name="modulation",
    )(cc, w_mod, b_mod.reshape(depth, 1, n))


def _norm_mod(x, g, modc_ref, modb_ref, row0, ctx_len, k_shift, k_scale):
    rows = x.shape[0]
    isctx = (row0 + lax.broadcasted_iota(jnp.int32, (rows, 1), 0)) < ctx_len
    sh = jnp.where(isctx, modc_ref[k_shift:k_shift + 1, :], modb_ref[k_shift:k_shift + 1, :])
    sc = jnp.where(isctx, modc_ref[k_scale:k_scale + 1, :], modb_ref[k_scale:k_scale + 1, :])
    ms = jnp.mean(x * x, axis=-1, keepdims=True)
    return (x * lax.rsqrt(ms + RMS_EPS) * g) * (1.0 + sc) + sh


def _gate_rows(modc_ref, modb_ref, row0, rows, ctx_len, k_gate):
    isctx = (row0 + lax.broadcasted_iota(jnp.int32, (rows, 1), 0)) < ctx_len
    return jnp.where(isctx, modc_ref[k_gate:k_gate + 1, :], modb_ref[k_gate:k_gate + 1, :])


def _inproj_kernel(x_ref, g_ref, modc_ref, modb_ref, w_ref, z_ref, *, tt, ctx_len):
    i = pl.program_id(2)
    h = _norm_mod(x_ref[...], g_ref[...], modc_ref, modb_ref, i * tt, ctx_len, 0, 1)
    z_ref[...] = _bdot(h, w_ref[...])


def _inproj(xa, g, modc, modb, w, tt, ctx_len):
    b, ta, d = xa.shape
    nh = 2
    tn = Z_W // nh
    return pl.pallas_call(
        functools.partial(_inproj_kernel, tt=tt, ctx_len=ctx_len),
        grid=(nh, b, ta // tt),
        in_specs=[pl.BlockSpec((None, tt, d), lambda n, bb, i: (bb, i, 0)),
                  pl.BlockSpec((1, d), lambda n, bb, i: (0, 0)),
                  pl.BlockSpec((6, d), lambda n, bb, i: (0, 0)),
                  pl.BlockSpec((None, 6, d), lambda n, bb, i: (bb, 0, 0)),
                  pl.BlockSpec((d, tn), lambda n, bb, i: (0, n))],
        out_specs=pl.BlockSpec((None, tt, tn), lambda n, bb, i: (bb, i, n)),
        out_shape=jax.ShapeDtypeStruct((b, ta, Z_W), F32),
        compiler_params=_params(("arbitrary",) * 3, 48 * 1024 * 1024),
        name="inproj",
    )(xa, g, modc, modb, w)


def _pool_kernel(zp_ref, zc_ref, zn_ref, pw_ref, ps_ref, o_ref, *, tt, ctx_len, ta):
    i = pl.program_id(1)
    zc = zc_ref[...]
    zcat = jnp.concatenate([zp_ref[...], zc, zn_ref[...]], axis=0)
    t_g = i * tt + lax.broadcasted_iota(jnp.int32, (tt, 1), 0)
    s_g = i * tt - POOL_HALO + lax.broadcasted_iota(jnp.int32, (tt, tt + 2 * POOL_HALO), 1)
    in_ctx = i * tt < ctx_len
    seg_lo = jnp.where(in_ctx, 0, ctx_len)
    seg_hi = jnp.where(in_ctx, ctx_len, ta)
    for gi, win in enumerate(POOL_WINDOWS):
        sl = slice(gi * POOL_GW, (gi + 1) * POOL_GW)
        lo = jnp.clip(t_g - win // 2, seg_lo, seg_hi)
        hi = jnp.clip(t_g - win // 2 + win, seg_lo, seg_hi)
        msk = jnp.where(s_g >= lo, jnp.where(s_g < hi, 1.0, 0.0), 0.0).astype(BF16)
        hi_b, lo_b = _split2(zcat[:, sl])
        wsum = jnp.dot(msk, hi_b, preferred_element_type=F32) + jnp.dot(msk, lo_b, preferred_element_type=F32)
        p = wsum / (hi - lo).astype(F32) - zc[:, sl]
        o_ref[:, sl] = _bdot(p, pw_ref[gi]) * ps_ref[:, sl]


def _pool(z, pool_w, pool_scale, tt, ctx_len):
    b, ta, _ = z.shape
    halo = POOL_HALO
    nb = ta // halo
    cb = COL_P // BRANCH_W
    return pl.pallas_call(
        functools.partial(_pool_kernel, tt=tt, ctx_len=ctx_len, ta=ta),
        grid=(b, ta // tt),
        in_specs=[pl.BlockSpec((None, halo, BRANCH_W), lambda bb, i: (bb, jnp.maximum(i * (tt // halo) - 1, 0), cb)),
                  pl.BlockSpec((None, tt, BRANCH_W), lambda bb, i: (bb, i, cb)),
                  pl.BlockSpec((None, halo, BRANCH_W), lambda bb, i: (bb, jnp.minimum((i + 1) * (tt // halo), nb - 1), cb)),
                  pl.BlockSpec((4, POOL_GW, POOL_GW), lambda bb, i: (0, 0, 0)),
                  pl.BlockSpec((1, BRANCH_W), lambda bb, i: (0, 0))],
        out_specs=pl.BlockSpec((None, tt, BRANCH_W), lambda bb, i: (bb, i, 0)),
        out_shape=jax.ShapeDtypeStruct((b, ta, BRANCH_W), F32),
        compiler_params=_params(("arbitrary",) * 2),
        name="pool",
    )(z, z, z, pool_w, pool_scale)


def _shifted(z, zprev, znext, mu, first, last):
    tt = z.shape[0]
    ridx = lax.broadcasted_iota(jnp.int32, (tt, 1), 0)
    zp = pltpu.roll(z, 1, 0)
    zp = jnp.where(ridx == 0, zprev[7:8, :], zp)
    zp = jnp.where(first, 0.0, zp)
    zn = pltpu.roll(z, tt - 1, 0)
    zn = jnp.where(ridx == tt - 1, znext[0:1, :], zn)
    zn = jnp.where(last, 0.0, zn)
    return z + mu[0:1, :] * (zp - z) + mu[1:2, :] * (zn - z)


def _prep_kernel(rp_ref, rc_ref, rn_ref, lp_ref, lc_ref, ln_ref, mur_ref, mul_ref, w0_ref, w2_ref, a0_ref, a2_ref,
                 g2_ref, kkw_ref, kaw_ref, brk_ref, bd_ref,
                 r_ref, v_ref, kk_ref, lw_ref, kd_ref, ka_ref, bon_ref, gate_ref, *, tt, ctx_len, ta):
    i = pl.program_id(1)
    grow = i * tt + lax.broadcasted_iota(jnp.int32, (tt, 1), 0)
    first = (grow == 0) | (grow == ctx_len)
    last = (grow == ctx_len - 1) | (grow == ta - 1)
    rkv = _shifted(rc_ref[...], rp_ref[...], rn_ref[...], mur_ref[...], first, last)
    lo = _shifted(lc_ref[...], lp_ref[...], ln_ref[...], mul_ref[...], first, last)
    c = BRANCH_W
    r, k, v = rkv[:, :c], rkv[:, c:2 * c], rkv[:, 2 * c:]
    y = _bdot(jnp.tanh(lo), w2_ref[...]) + w0_ref[...]
    lw = -jnp.exp(-0.5) * jax.nn.sigmoid(y)
    a = jax.nn.sigmoid(_bdot(lo, a2_ref[...]) + a0_ref[...])
    gate_ref[...] = _bdot(jax.nn.sigmoid(lo), g2_ref[...])
    bd = bd_ref[...]
    kk = k * kkw_ref[...]
    ss = _dot2(kk * kk, bd)
    kk = kk / jnp.maximum(jnp.sqrt(ss), 1e-12)
    a_f, a_b = a[:, :c], a[:, c:]
    kaw = kaw_ref[...]
    kd_f = k * (1.0 + (a_f - 1.0) * kaw)
    kd_b = k * (1.0 + (a_b - 1.0) * kaw)
    r_ref[...] = r
    v_ref[...] = v.astype(BF16)
    kk_ref[...] = kk
    lw_ref[...] = lw
    kd_ref[:, :c] = kd_f
    kd_ref[:, c:] = kd_b
    ka_ref[:, :c] = a_f * kk
    ka_ref[:, c:] = a_b * kk
    bon_ref[...] = _dot2(r * (kd_f + kd_b) * brk_ref[...], bd) * v


def _prep(z, mur, mul, w0, w2f, a0, a2f, g2f, kkw, kaw, brk, bd, tt, ctx_len):
    b, ta, _ = z.shape
    nb = ta // 8
    c = BRANCH_W
    rb, lb = COL_RKV // (3 * c), COL_LORA // LORA_W
    prev = lambda bb, i: (bb, jnp.maximum(i * (tt // 8) - 1, 0))
    nxt = lambda bb, i: (bb, jnp.minimum((i + 1) * (tt // 8), nb - 1))
    full = lambda shape: pl.BlockSpec(shape, lambda bb, i: (0,) * len(shape))
    tok = lambda w: pl.BlockSpec((None, tt, w), lambda bb, i: (bb, i, 0))
    sds = lambda w: jax.ShapeDtypeStruct((b, ta, w), F32)
    return pl.pallas_call(
        functools.partial(_prep_kernel, tt=tt, ctx_len=ctx_len, ta=ta),
        grid=(b, ta // tt),
        in_specs=[pl.BlockSpec((None, 8, 3 * c), lambda bb, i: prev(bb, i) + (rb,)),
                  pl.BlockSpec((None, tt, 3 * c), lambda bb, i: (bb, i, rb)),
                  pl.BlockSpec((None, 8, 3 * c), lambda bb, i: nxt(bb, i) + (rb,)),
                  pl.BlockSpec((None, 8, LORA_W), lambda bb, i: prev(bb, i) + (lb,)),
                  pl.BlockSpec((None, tt, LORA_W), lambda bb, i: (bb, i, lb)),
                  pl.BlockSpec((None, 8, LORA_W), lambda bb, i: nxt(bb, i) + (lb,)),
                  full((2, 3 * c)), full((2, LORA_W)), full((1, 2 * c)), full((LORA_W, 2 * c)), full((1, 2 * c)),
                  full((LORA_W, 2 * c)), full((LORA_W, c)), full((1, c)), full((1, c)), full((1, c)), full((c, c))],
        out_specs=[tok(c), tok(c), tok(c), tok(2 * c), tok(2 * c), tok(2 * c), tok(c), tok(c)],
        out_shape=[sds(c), jax.ShapeDtypeStruct((b, ta, c), BF16), sds(c), sds(2 * c), sds(2 * c), sds(2 * c),
                   sds(c), sds(c)],
        compiler_params=_params(("arbitrary",) * 2, 40 * 1024 * 1024),
        name="rwkv_prep",
    )(z, z, z, z, z, z, mur, mul, w0, w2f, a0, a2f, g2f, kkw, kaw, brk, bd)


def _wkv_pre_kernel(r_ref, v_ref, kk_ref, lw_ref, kd_ref, ka_ref,
                    kqt_ref, rqt_ref, u0_ref, o0_ref, xt_ref, pt_ref, *, g_chunks):
    d = pl.program_id(1)
    C = CHUNK
    n = g_chunks * C
    sgn = 1 - 2 * d
    row = lax.broadcasted_iota(jnp.int32, (n, n), 0)
    col = lax.broadcasted_iota(jnp.int32, (n, n), 1)
    same = (row // C) == (col // C)
    tri = jnp.where(same, jnp.where((col - row) * sgn <= 0, 1.0, 0.0), 0.0).astype(BF16)
    lw = lw_ref[...]
    cs = _ldot3(tri, lw)
    kd = kd_ref[...]
    ka = ka_ref[...]
    kq_all = kk_ref[...] * jnp.exp(cs - lw)
    rq_all = r_ref[...] * jnp.exp(cs)
    p_inv = jnp.exp(-cs)
    kdi_all = kd * p_inv
    kai_all = ka * p_inv
    v_all = v_ref[...]

    t_i = lax.broadcasted_iota(jnp.int32, (C, 2 * C), 0)
    s_i = lax.broadcasted_iota(jnp.int32, (C, 2 * C), 1) & (C - 1)
    strict = (s_i - t_i) * sgn < 0
    incl = (s_i - t_i) * sgn <= 0
    lane = lax.broadcasted_iota(jnp.int32, (1, 2 * HD), 1)
    m1 = jnp.where(lane < HD, 1.0, 0.0)
    m2 = 1.0 - m1
    r2 = lax.broadcasted_iota(jnp.int32, (2 * HD, 2 * HD), 0)
    c2 = lax.broadcasted_iota(jnp.int32, (2 * HD, 2 * HD), 1)
    eye = jnp.where(r2 == c2, 1.0, 0.0)
    stack2 = lambda x: jnp.concatenate([x * m1, x * m2], axis=0)

    for gc in range(g_chunks):
        rows = slice(gc * C, (gc + 1) * C)
        cs_c = cs[rows]
        tot = jnp.where(d == 1, cs_c[0:1, :], cs_c[C - 1:C, :])
        p_end = jnp.exp(tot - cs_c)
        pt_ref[gc] = jnp.broadcast_to(jnp.exp(tot), (8, tot.shape[-1]))
        kde = kd[rows] * p_end
        kae = ka[rows] * p_end
        for p in range(HEADS // 2):
            sl = slice(2 * HD * p, 2 * HD * (p + 1))
            xt_ref[gc, p] = jnp.concatenate([kde[:, sl], -kae[:, sl]], axis=0).T.astype(BF16)

    units = [(slice(gc * C, (gc + 1) * C), slice(2 * HD * p, 2 * HD * (p + 1)))
             for gc in range(g_chunks) for p in range(HEADS // 2)]
    each = lambda f, *lists: [f(*a) for a in zip(*lists)]
    kq = [kq_all[rows, sl] for rows, sl in units]
    rq = [rq_all[rows, sl] for rows, sl in units]
    g = [_bdot_nt(jnp.concatenate([kq[n_], rq[n_]], axis=0),
                  jnp.concatenate([stack2(kdi_all[rows, sl]), stack2(kai_all[rows, sl])], axis=0))
         for n_, (rows, sl) in enumerate(units)]
    akk = [jnp.where(strict, x[:C, :2 * C], 0.0) for x in g]
    ark = [jnp.where(incl, x[C:, :2 * C], 0.0) for x in g]
    ara = [jnp.where(incl, x[C:, 2 * C:], 0.0) for x in g]
    lmat = [stack2(jnp.where(strict, x[:C, 2 * C:], 0.0)) for x in g]
    tm = [eye - x for x in lmat]
    pw = each(_bdot, lmat, lmat)
    n_sq = CHUNK.bit_length() - 2
    for it in range(n_sq):
        tm = each(lambda t_, p_: t_ + _bdot(t_, p_), tm, pw)
        if it + 1 < n_sq:
            pw = each(_bdot, pw, pw)
    tss = [x[:C, :] + x[C:, :] for x in tm]
    vst = [stack2(v_all[rows, sl].astype(F32)) for rows, sl in units]
    kqt = each(lambda t_, k_: _bdot(t_, stack2(k_)), tss, kq)
    akv = each(_bdot, akk, vst)
    u0 = each(lambda t_, a_: _bdot(t_, stack2(a_)), tss, akv)
    rqt = each(lambda r_, a_, k_: r_ - _bdot(a_, stack2(k_)), rq, ara, kqt)
    o0 = each(lambda ak_, aa_, v_, u_: _bdot(jnp.concatenate([ak_, -aa_], axis=1),
                                             jnp.concatenate([v_, stack2(u_)], axis=0)), ark, ara, vst, u0)
    for n_, (rows, sl) in enumerate(units):
        kqt_ref[rows, sl] = kqt[n_].astype(BF16)
        rqt_ref[rows, sl] = rqt[n_].astype(BF16)
        u0_ref[rows, sl] = u0[n_]
        o0_ref[rows, sl] = o0[n_]


def _wkv_seq_kernel(*refs, n_batch, cps):
    ins, (of_ref, ob_ref), st_ref = refs[:14], refs[14:16], refs[16]
    ci = pl.program_id(0)
    C = CHUNK

    @pl.when(ci == 0)
    def _():
        st_ref[...] = jnp.zeros_like(st_ref)

    r2 = lax.broadcasted_iota(jnp.int32, (2 * HD, 2 * HD), 0)
    c2 = lax.broadcasted_iota(jnp.int32, (2 * HD, 2 * HD), 1)
    bdm = (r2 < HD) == (c2 < HD)
    units = [(d, bb, p) for d in range(2) for bb in range(n_batch) for p in range(HEADS // 2)]
    st = [st_ref[d, bb, p] for d, bb, p in units]
    for q_ in range(cps):
        uo, u = [], []
        for n_, (d, bb, p) in enumerate(units):
            kqt_ref, rqt_ref = ins[7 * d], ins[7 * d + 1]
            q = q_ if d == 0 else cps - 1 - q_
            rows, sl = slice(q * C, (q + 1) * C), slice(2 * HD * p, 2 * HD * (p + 1))
            lhs = jnp.concatenate([kqt_ref[bb, rows, sl], rqt_ref[bb, rows, sl]], axis=0)
            uo.append(jnp.dot(lhs, st[n_].astype(BF16), preferred_element_type=F32))
        for n_, (d, bb, p) in enumerate(units):
            u0_ref, o0_ref = ins[7 * d + 2], ins[7 * d + 3]
            o_ref = of_ref if d == 0 else ob_ref
            q = q_ if d == 0 else cps - 1 - q_
            rows, sl = slice(q * C, (q + 1) * C), slice(2 * HD * p, 2 * HD * (p + 1))
            u.append(uo[n_][:C, :] + u0_ref[bb, rows, sl])
            o_ref[bb, rows, sl] = uo[n_][C:, :] + o0_ref[bb, rows, sl]
        for n_, (d, bb, p) in enumerate(units):
            xt_ref, pt_ref, v_ref = ins[7 * d + 4], ins[7 * d + 5], ins[7 * d + 6]
            q = q_ if d == 0 else cps - 1 - q_
            rows, sl = slice(q * C, (q + 1) * C), slice(2 * HD * p, 2 * HD * (p + 1))
            yv = jnp.concatenate([v_ref[bb, rows, sl], u[n_].astype(BF16)], axis=0)
            upd = jnp.dot(xt_ref[bb, q, p], yv, preferred_element_type=F32)
            pcol = jnp.broadcast_to(pt_ref[bb, q, 0:1, sl], (2 * HD, 2 * HD)).T
            st[n_] = st[n_] * pcol + jnp.where(bdm, upd, 0.0)
    for n_, (d, bb, p) in enumerate(units):
        st_ref[d, bb, p] = st[n_]


def _wkv(r, v, kk, lw, kd, ka, ctx_len):
    b, ta, c = r.shape
    nch = ta // CHUNK
    ncc = ctx_len // CHUNK
    gch = 4
    assert nch % gch == 0
    n = gch * CHUNK
    npair = HEADS // 2
    shared = pl.BlockSpec((None, n, c), lambda bb, d, i: (bb, i, 0))
    per_dir = pl.BlockSpec((None, n, c), lambda bb, d, i: (bb, i, d))
    tok = pl.BlockSpec((None, None, n, c), lambda bb, d, i: (d, bb, i, 0))
    kqt, rqt, u0, o0, xt, pt = pl.pallas_call(
        functools.partial(_wkv_pre_kernel, g_chunks=gch),
        grid=(b, 2, nch // gch),
        in_specs=[shared, shared, shared, per_dir, per_dir, per_dir],
        out_specs=[tok, tok, tok, tok,
                   pl.BlockSpec((None, None, gch, npair, 2 * HD, 2 * CHUNK), lambda bb, d, i: (d, bb, i, 0, 0, 0)),
                   pl.BlockSpec((None, None, gch, 8, c), lambda bb, d, i: (d, bb, i, 0, 0))],
        out_shape=[jax.ShapeDtypeStruct((2, b, ta, c), BF16), jax.ShapeDtypeStruct((2, b, ta, c), BF16),
                   jax.ShapeDtypeStruct((2, b, ta, c), F32), jax.ShapeDtypeStruct((2, b, ta, c), F32),
                   jax.ShapeDtypeStruct((2, b, nch, npair, 2 * HD, 2 * CHUNK), BF16),
                   jax.ShapeDtypeStruct((2, b, nch, 8, c), F32)],
        compiler_params=_params(("arbitrary",) * 3, 40 * 1024 * 1024),
        name="wkv_pre",
    )(r, v, kk, lw, kd, ka)

    cps = 2
    assert nch % cps == 0 and ncc % cps == 0
    nsb, ncb = nch // cps, ncc // cps

    def blk(d, ci):
        return ci if d == 0 else jnp.where(ci < ncb, ncb - 1 - ci, nsb + ncb - 1 - ci)

    in_specs, args = [], []
    for d in range(2):
        tokd = pl.BlockSpec((None, b, cps * CHUNK, c), lambda ci, d=d: (d, 0, blk(d, ci), 0))
        in_specs += [tokd, tokd, tokd, tokd,
                     pl.BlockSpec((None, b, cps, npair, 2 * HD, 2 * CHUNK),
                                  lambda ci, d=d: (d, 0, blk(d, ci), 0, 0, 0)),
                     pl.BlockSpec((None, b, cps, 8, c), lambda ci, d=d: (d, 0, blk(d, ci), 0, 0)),
                     pl.BlockSpec((b, cps * CHUNK, c), lambda ci, d=d: (0, blk(d, ci), 0))]
        args += [kqt, rqt, u0, o0, xt, pt, v]
    return pl.pallas_call(
        functools.partial(_wkv_seq_kernel, n_batch=b, cps=cps),
        grid=(nsb,),
        in_specs=in_specs,
        out_specs=[pl.BlockSpec((b, cps * CHUNK, c), lambda ci, d=d: (0, blk(d, ci), 0)) for d in range(2)],
        out_shape=[jax.ShapeDtypeStruct((b, ta, c), F32)] * 2,
        scratch_shapes=[pltpu.VMEM((2, b, npair, 2 * HD, 2 * HD), F32)],
        compiler_params=_params(("arbitrary",)),
        name="wkv_seq",
    )(*args)


def _qknorm_kernel(q_ref, k_ref, v_ref, gq_ref, gk_ref, bd_ref, qo_ref, ko_ref, vo_ref):
    bd = bd_ref[...]

    def norm(x, g):
        ms = _dot2(x * x, bd) * (1.0 / HD)
        return x * lax.rsqrt(ms + RMS_EPS) * g

    qo_ref[...] = (norm(q_ref[...], gq_ref[...]) * NAT_SCALE).astype(BF16)
    ko_ref[...] = norm(k_ref[...], gk_ref[...]).astype(BF16)
    vo_ref[...] = v_ref[...].astype(BF16)


def _qknorm(z, gq, gk, bd, tt):
    b, ta, _ = z.shape
    c = BRANCH_W
    col = lambda cb: pl.BlockSpec((None, tt, c), lambda bb, i: (bb, i, cb))
    full = lambda shape: pl.BlockSpec(shape, lambda bb, i: (0,) * len(shape))
    out = pl.BlockSpec((None, tt, c), lambda bb, i: (bb, i, 0))
    sds = jax.ShapeDtypeStruct((b, ta, c), BF16)
    return pl.pallas_call(
        _qknorm_kernel,
        grid=(b, ta // tt),
        in_specs=[col(COL_Q // c), col(COL_K // c), col(COL_V // c), full((1, c)), full((1, c)), full((c, c))],
        out_specs=[out, out, out],
        out_shape=[sds, sds, sds],
        compiler_params=_params(("arbitrary",) * 2),
        name="nat_qknorm",
    )(z, z, z, gq, gk, bd)


def _nat_kernel(q_ref, kp_ref, kc_ref, kn_ref, vp_ref, vc_ref, vn_ref, kx_ref, vx_ref, bias_ref, o_ref,
                kbuf, vbuf, *, tt):
    i = pl.program_id(1)
    lane = lax.broadcasted_iota(jnp.int32, (1, 2 * HD), 1)
    first = lane < HD

    def softmax_pv(parts):
        mx = None
        for s, _ in parts:
            m = jnp.max(s, axis=-1, keepdims=True)
            mx = m if mx is None else jnp.maximum(mx, m)
        den = 0.0
        acc = 0.0
        for s, vv in parts:
            e = jnp.exp(s - mx)
            den = den + jnp.sum(e, axis=-1, keepdims=True)
            acc = acc + _bdot(e, vv)
        return acc / den

    @pl.when(i == 0)
    def _():
        for p in range(HEADS // 2):
            sl = slice(2 * HD * p, 2 * HD * (p + 1))
            qp = q_ref[:, sl]
            kx = kx_ref[:, sl]
            vx = vx_ref[:, sl]
            outs = []
            for h in range(2):
                qm = jnp.where(first if h == 0 else ~first, qp, jnp.zeros_like(qp))
                outs.append(softmax_pv([(_bdot_nt(qm, kx), vx)]))
            o_ref[:, sl] = jnp.where(first, outs[0], outs[1])

    @pl.when(i > 0)
    def _():
        for n_, (kr, vr) in enumerate(((kp_ref, vp_ref), (kc_ref, vc_ref), (kn_ref, vn_ref), (kx_ref, vx_ref))):
            kbuf[n_ * tt:(n_ + 1) * tt, :] = kr[...]
            vbuf[n_ * tt:(n_ + 1) * tt, :] = vr[...]

        def scores(hd):
            sl = slice(2 * HD * (hd // 2), 2 * HD * (hd // 2 + 1))
            qp = q_ref[:, sl]
            qm = jnp.where(first if hd % 2 == 0 else ~first, qp, jnp.zeros_like(qp))
            return _bdot_nt(qm, kbuf[:, sl])

        s_next = scores(0)
        outs = []
        for hd in range(HEADS):
            s = s_next
            if hd + 1 < HEADS:
                s_next = scores(hd + 1)
            sl = slice(2 * HD * (hd // 2), 2 * HD * (hd // 2 + 1))
            s_loc = s[:, :3 * tt] + bias_ref[hd]
            s_ctx = s[:, 3 * tt:]
            mx = jnp.maximum(jnp.max(s_loc, axis=-1, keepdims=True), jnp.max(s_ctx, axis=-1, keepdims=True))
            e = jnp.concatenate([jnp.exp(s_loc - mx), jnp.exp(s_ctx - mx)], axis=1)
            den = jnp.sum(e, axis=-1, keepdims=True)
            outs.append(_bdot(e, vbuf[:, sl]) / den)
            if hd % 2 == 1:
                o_ref[:, sl] = jnp.where(first, outs[hd - 1], outs[hd])


def _nat(qn, kn, vb, bias3, tt, ctx_len):
    b, ta, c = qn.shape
    nt = ta // tt
    cur = pl.BlockSpec((None, tt, c), lambda bb, i: (bb, i, 0))
    prv = pl.BlockSpec((None, tt, c), lambda bb, i: (bb, jnp.maximum(i - 1, 0), 0))
    nxt = pl.BlockSpec((None, tt, c), lambda bb, i: (bb, jnp.minimum(i + 1, nt - 1), 0))
    cx = pl.BlockSpec((None, tt, c), lambda bb, i: (bb, 0, 0))
    variant = lambda i: jnp.where(i <= 1, 0, jnp.where(i == nt - 1, 2, 1))
    return pl.pallas_call(
        functools.partial(_nat_kernel, tt=tt),
        grid=(b, nt),
        in_specs=[cur, prv, cur, nxt, prv, cur, nxt, cx, cx,
                  pl.BlockSpec((None, HEADS, tt, 3 * tt), lambda bb, i: (variant(i), 0, 0, 0))],
        out_specs=pl.BlockSpec((None, tt, c), lambda bb, i: (bb, i, 0)),
        out_shape=jax.ShapeDtypeStruct((b, ta, c), F32),
        scratch_shapes=[pltpu.VMEM((4 * tt, c), BF16), pltpu.VMEM((4 * tt, c), BF16)],
        compiler_params=_params(("arbitrary",) * 2, 48 * 1024 * 1024),
        name="nat",
    )(qn, kn, kn, kn, vb, vb, vb, kn, vb, bias3)


def _nat_bias_table(rpb, tt):
    h = rpb.shape[0]
    rpt = tt // GRID_W
    cols = jnp.arange(GRID_W)
    cstart = jnp.clip(cols - NAT_KW // 2, 0, GRID_W - NAT_KW)
    kc = cols[None, :]
    col_ok = (kc >= cstart[:, None]) & (kc < cstart[:, None] + NAT_KW)
    idx = jnp.clip(kc - cols[:, None] + (NAT_KW - 1), 0, 2 * NAT_KW - 2)
    t15 = jnp.where(col_ok[None, None], rpb[:, :, idx], NEG).astype(F32)
    a = jnp.arange(rpt)[:, None]
    j = jnp.arange(3 * rpt)[None, :]
    d = j - a - rpt + (NAT_KH - 1)
    blocks = t15[:, d]
    lo = jnp.stack([jnp.full((rpt,), rpt), a[:, 0] + rpt - NAT_KH // 2, jnp.full((rpt,), 2 * rpt - NAT_KH)])
    row_ok = (j[None] >= lo[:, :, None]) & (j[None] < lo[:, :, None] + NAT_KH)
    full = jnp.where(row_ok[:, None, :, :, None, None], blocks[None], NEG)
    return full.transpose(0, 1, 2, 4, 3, 5).reshape(3, h, tt, 3 * tt)


def _merge_kernel(x_ref, pool_ref, of_ref, ob_ref, bon_ref, gate_ref, nat_ref, zg_ref, modc_ref, modb_ref,
                  gng_ref, gnb_ref, bd_ref, wb_ref, wo_ref, o_ref, *, tt, ctx_len):
    i = pl.program_id(1)
    bd = bd_ref[...]
    wkv = of_ref[...] + ob_ref[...]
    mu = _dot2(wkv, bd) * (1.0 / HD)
    cen = wkv - mu
    var = _dot2(cen * cen, bd) * (1.0 / HD)
    y = cen * lax.rsqrt(var + GN_EPS) * gng_ref[...] + gnb_ref[...]
    rw = (y + bon_ref[...]) * gate_ref[...]
    d = x_ref.shape[-1]
    m = jnp.zeros((tt, d), F32)
    for n, br in enumerate((pool_ref[...], rw, nat_ref[...])):
        m = m + jax.nn.sigmoid(zg_ref[:, n * d:(n + 1) * d]) * _bdot(br, wb_ref[n])
    yout = _bdot(m, wo_ref[...])
    o_ref[...] = x_ref[...] + _gate_rows(modc_ref, modb_ref, i * tt, tt, ctx_len, 2) * yout


def _merge(xa, pool, o2, bonus, gate, nat, z, modc, modb, gng, gnb, bd, wb, wo, tt, ctx_len):
    b, ta, d = xa.shape
    c = BRANCH_W
    tok = lambda w: pl.BlockSpec((None, tt, w), lambda bb, i: (bb, i, 0))
    full = lambda shape: pl.BlockSpec(shape, lambda bb, i: (0,) * len(shape))
    return pl.pallas_call(
        functools.partial(_merge_kernel, tt=tt, ctx_len=ctx_len),
        grid=(b, ta // tt),
        in_specs=[tok(d), tok(c), tok(c), tok(c), tok(c), tok(c), tok(c),
                  pl.BlockSpec((None, tt, ZG_W), lambda bb, i: (bb, i, COL_G // ZG_W)),
                  full((6, d)), pl.BlockSpec((None, 6, d), lambda bb, i: (bb, 0, 0)),
                  full((1, c)), full((1, c)), full((c, c)), full((3, c, d)), full((d, d))],
        out_specs=tok(d),
        out_shape=jax.ShapeDtypeStruct((b, ta, d), F32),
        compiler_params=_params(("arbitrary",) * 2, 48 * 1024 * 1024),
        name="merge",
    )(xa, pool, o2[0], o2[1], bonus, gate, nat, z, modc, modb, gng, gnb, bd, wb, wo)


def _ffn_kernel(x_ref, g_ref, modc_ref, modb_ref, w1_ref, w3_ref, w2_ref, o_ref, h_scr, acc, *, tm, ctx_len):
    i = pl.program_id(1)
    f = pl.program_id(2)

    @pl.when(f == 0)
    def _():
        h_scr[...] = _norm_mod(x_ref[...], g_ref[...], modc_ref, modb_ref, i * tm, ctx_len, 3, 4).astype(BF16)
        acc[...] = jnp.zeros_like(acc)

    h = h_scr[...]
    a = jnp.dot(h, w1_ref[...], preferred_element_type=F32)
    bb = jnp.dot(h, w3_ref[...], preferred_element_type=F32)
    acc[...] += _bdot(_silu(a) * bb, w2_ref[...])

    @pl.when(f == pl.num_programs(2) - 1)
    def _():
        o_ref[...] = x_ref[...] + _gate_rows(modc_ref, modb_ref, i * tm, tm, ctx_len, 5) * acc[...]


def _ffn(xa, g, modc, modb, w1, w3, w2, tm, fc, ctx_len):
    b, ta, d = xa.shape
    dff = w1.shape[1]
    tok = pl.BlockSpec((None, tm, d), lambda bb, i, f: (bb, i, 0))
    return pl.pallas_call(
        functools.partial(_ffn_kernel, tm=tm, ctx_len=ctx_len),
        grid=(b, ta // tm, dff // fc),
        in_specs=[tok,
                  pl.BlockSpec((1, d), lambda bb, i, f: (0, 0)),
                  pl.BlockSpec((6, d), lambda bb, i, f: (0, 0)),
                  pl.BlockSpec((None, 6, d), lambda bb, i, f: (bb, 0, 0)),
                  pl.BlockSpec((d, fc), lambda bb, i, f: (0, f)),
                  pl.BlockSpec((d, fc), lambda bb, i, f: (0, f)),
                  pl.BlockSpec((fc, d), lambda bb, i, f: (f, 0))],
        out_specs=tok,
        out_shape=jax.ShapeDtypeStruct((b, ta, d), F32),
        scratch_shapes=[pltpu.VMEM((tm, d), BF16), pltpu.VMEM((tm, d), F32)],
        compiler_params=_params(("arbitrary",) * 3, 48 * 1024 * 1024),
        name="ffn",
    )(xa, g, modc, modb, w1, w3, w2)


def _route_kernel(x_ref, g_ref, modc_ref, modb_ref, rt_ref, h_ref, r_ref, *, tm, ctx_len, n_exp):
    i = pl.program_id(1)
    lane = lax.broadcasted_iota(jnp.int32, (tm, 128), 1)
    h = _norm_mod(x_ref[...], g_ref[...], modc_ref, modb_ref, i * tm, ctx_len, 3, 4)
    h_ref[...] = h
    logits = jnp.dot(h, rt_ref[...], precision=lax.Precision.HIGHEST, preferred_element_type=F32)
    logits = jnp.where(lane < n_exp, logits, NEG)
    m1 = jnp.max(logits, axis=-1, keepdims=True)
    i1 = jnp.min(jnp.where(logits == m1, lane, 128), axis=-1, keepdims=True)
    rest = jnp.where(lane == i1, NEG, logits)
    m2 = jnp.max(rest, axis=-1, keepdims=True)
    i2 = jnp.min(jnp.where(rest == m2, lane, 128), axis=-1, keepdims=True)
    e2 = jnp.exp(m2 - m1)
    den = 1.0 + e2
    r_ref[...] = jnp.where(lane == 0, i1.astype(F32),
                           jnp.where(lane == 1, i2.astype(F32), jnp.where(lane == 2, 1.0 / den, e2 / den)))


def _route(xa, g, modc, modb, router, tm, ctx_len, n_exp):
    b, ta, d = xa.shape
    tok = lambda w: pl.BlockSpec((None, tm, w), lambda bb, i: (bb, i, 0))
    return pl.pallas_call(
        functools.partial(_route_kernel, tm=tm, ctx_len=ctx_len, n_exp=n_exp),
        grid=(b, ta // tm),
        in_specs=[tok(d),
                  pl.BlockSpec((1, d), lambda bb, i: (0, 0)),
                  pl.BlockSpec((6, d), lambda bb, i: (0, 0)),
                  pl.BlockSpec((None, 6, d), lambda bb, i: (bb, 0, 0)),
                  pl.BlockSpec((d, 128), lambda bb, i: (0, 0))],
        out_specs=[tok(d), tok(128)],
        out_shape=[jax.ShapeDtypeStruct((b, ta, d), F32), jax.ShapeDtypeStruct((b, ta, 128), F32)],
        compiler_params=_params(("arbitrary",) * 2, 40 * 1024 * 1024),
        name="moe_route",
    )(xa, g, modc, modb, router)


def _experts_kernel(te_ref, nact_ref, src_ref, dst_ref, h_hbm, wrow_ref, w1_ref, w3_ref, w2_ref, y_hbm,
                    hbuf, hsel, acc, gsem, ssem, *, ts):
    j = pl.program_id(0)
    f = pl.program_id(1)
    nj = pl.num_programs(0)
    nf = pl.num_programs(1)
    slot = j % 2

    def gather_copy(tile, sl_, r):
        return pltpu.make_async_copy(h_hbm.at[pl.ds(src_ref[tile * ts + r], 1)], hbuf.at[sl_, pl.ds(r, 1)],
                                     gsem.at[sl_])

    def scatter_copy(tile, r):
        return pltpu.make_async_copy(acc.at[pl.ds(r, 1)], y_hbm.at[pl.ds(dst_ref[tile * ts + r], 1)], ssem.at[0])

    def for_rows(fn):
        def body(r, carry):
            fn(r)
            return carry
        lax.fori_loop(0, ts, body, 0, unroll=8)

    @pl.when(f == 0)
    def _():
        @pl.when(j == 0)
        def _():
            for_rows(lambda r: gather_copy(j, slot, r).start())

        for_rows(lambda r: gather_copy(j, slot, r).wait())
        hsel[...] = hbuf[slot].astype(BF16)

        @pl.when(j > 0)
        def _():
            for_rows(lambda r: scatter_copy(j - 1, r).wait())

        acc[...] = jnp.zeros_like(acc)

    @pl.when((f == nf - 1) & (j + 1 < nj))
    def _():
        for_rows(lambda r: gather_copy(j + 1, 1 - slot, r).start())

    @pl.when(j < nact_ref[0])
    def _():
        h = hsel[...]
        a = jnp.dot(h, w1_ref[...], preferred_element_type=F32)
        bb = jnp.dot(h, w3_ref[...], preferred_element_type=F32)
        acc[...] += _bdot(_silu(a) * bb * wrow_ref[...], w2_ref[...])

    @pl.when(f == nf - 1)
    def _():
        for_rows(lambda r: scatter_copy(j, r).start())

        @pl.when(j == nj - 1)
        def _():
            for_rows(lambda r: scatter_copy(j, r).wait())


def _experts(h2, te, nact, src, dst, wrow, w1, w3, w2, ts, fc, n_out_rows):
    n, d = h2.shape
    n_exp, _, dff = w1.shape
    nt = src.shape[0] // ts
    return pl.pallas_call(
        functools.partial(_experts_kernel, ts=ts),
        grid_spec=pltpu.PrefetchScalarGridSpec(
            num_scalar_prefetch=4,
            grid=(nt, dff // fc),
            in_specs=[pl.BlockSpec(memory_space=pl.ANY),
                      pl.BlockSpec((ts, 1), lambda j, f, te_, na_, s_, d_: (j, 0)),
                      pl.BlockSpec((None, d, fc), lambda j, f, te_, na_, s_, d_: (te_[j], 0, f)),
                      pl.BlockSpec((None, d, fc), lambda j, f, te_, na_, s_, d_: (te_[j], 0, f)),
                      pl.BlockSpec((None, fc, d), lambda j, f, te_, na_, s_, d_: (te_[j], f, 0))],
            out_specs=pl.BlockSpec(memory_space=pl.ANY),
            scratch_shapes=[pltpu.VMEM((2, ts, d), F32), pltpu.VMEM((ts, d), BF16), pltpu.VMEM((ts, d), F32),
                            pltpu.SemaphoreType.DMA((2,)), pltpu.SemaphoreType.DMA((1,))]),
        out_shape=jax.ShapeDtypeStruct((n_out_rows, d), F32),
        compiler_params=_params(("arbitrary",) * 2, 52 * 1024 * 1024, disable_bounds_checks=True),
        name="moe_experts",
    )(te, nact, src, dst, h2, wrow, w1, w3, w2)


def _combine_kernel(x_ref, y0_ref, y1_ref, modc_ref, modb_ref, o_ref, *, tt, ctx_len, skip):
    i = pl.program_id(1) + skip
    o_ref[...] = x_ref[...] + _gate_rows(modc_ref, modb_ref, i * tt, tt, ctx_len, 5) * (y0_ref[...] + y1_ref[...])


def _combine(xa, y, modc, modb, tt, ctx_len, latent_only):
    b, ta, d = xa.shape
    nt = ta // tt
    skip = ctx_len // tt if latent_only else 0
    return pl.pallas_call(
        functools.partial(_combine_kernel, tt=tt, ctx_len=ctx_len, skip=skip),
        grid=(b, nt - skip),
        in_specs=[pl.BlockSpec((None, tt, d), lambda bb, i: (bb, i + skip, 0)),
                  pl.BlockSpec((tt, d), lambda bb, i: (bb * nt + i + skip, 0)),
                  pl.BlockSpec((tt, d), lambda bb, i: (b * nt + bb * nt + i + skip, 0)),
                  pl.BlockSpec((6, d), lambda bb, i: (0, 0)),
                  pl.BlockSpec((None, 6, d), lambda bb, i: (bb, 0, 0))],
        out_specs=pl.BlockSpec((None, tt, d), lambda bb, i: (bb, i, 0)),
        out_shape=jax.ShapeDtypeStruct((b, ta - skip * tt, d), F32),
        compiler_params=_params(("arbitrary",) * 2),
        name="moe_combine",
    )(xa, y, y, modc, modb)


def _moe(xa, g, modc, modb, router, w1, w3, w2, tm, ctx_len, latent_only):
    b, ta, d = xa.shape
    n_exp = router.shape[-1]
    n = b * ta
    ts = MOE_TILE
    rt = jnp.zeros((d, 128), F32).at[:, :n_exp].set(router)
    h, route = _route(xa, g, modc, modb, rt, tm, ctx_len, n_exp)
    route = route.reshape(n, 128)
    e_s = route[:, :TOP_K].astype(jnp.int32).reshape(-1)
    w_s = route[:, TOP_K:2 * TOP_K].reshape(-1)
    onehot = (e_s[:, None] == jnp.arange(n_exp)[None, :]).astype(jnp.int32)
    cum = jnp.cumsum(onehot, axis=0)
    rank = jnp.take_along_axis(cum, e_s[:, None], axis=1)[:, 0] - 1
    padded = ((cum[-1] + ts - 1) // ts) * ts
    ends = jnp.cumsum(padded)
    pos = (ends - padded)[e_s] + rank
    p_rows = TOP_K * n + n_exp * ts
    inv = jnp.full((p_rows,), -1, jnp.int32).at[pos].set(jnp.arange(TOP_K * n, dtype=jnp.int32))
    real = inv >= 0
    row = jnp.maximum(inv, 0)
    src = row // TOP_K
    spare = TOP_K * n - 1 + jnp.cumsum(jnp.where(real, 0, 1))
    dst = jnp.where(real, (row % TOP_K) * n + row // TOP_K, spare).astype(jnp.int32)
    wrow = jnp.where(real, w_s[row], 0.0).reshape(p_rows, 1)
    tile_start = jnp.arange(p_rows // ts, dtype=jnp.int32) * ts
    te = jnp.minimum(jnp.sum(tile_start[:, None] >= ends[None, :], axis=1), n_exp - 1).astype(jnp.int32)
    nact = (ends[-1:] // ts).astype(jnp.int32)
    fc = w1.shape[-1] // 2
    y = _experts(h.reshape(n, d), te, nact, src, dst, wrow, w1, w3, w2, ts, fc, p_rows)
    return _combine(xa, y, modc, modb, TOK_TILE, ctx_len, latent_only)


def _layer_weights(li, w_in, pool_w, pool_scale, shift_mu, decay_w0, decay_w2, iclr_a0, iclr_a2, key_kk, key_ka,
                   bonus_rk, gn_g, gn_b, gate_g2, nat_qn_g, nat_kn_g, nat_rpb, w_branch, w_out):
    c = BRANCH_W
    d = w_in.shape[1]
    wi = w_in[li]
    o_r = c
    o_lora = o_r + 3 * c
    n_lora = 2 * DECAY_LORA + 2 * ICLR_LORA + GATE_LORA
    o_q = o_lora + n_lora
    o_g = o_q + 3 * c
    pad = jnp.zeros((d, LORA_W - n_lora), F32)
    w_z = jnp.concatenate([wi[:, o_g:], wi[:, o_r:o_lora], wi[:, o_q:o_g], wi[:, :c], wi[:, o_lora:o_q], pad],
                          axis=1).astype(BF16)
    mu = shift_mu[li]
    mur = mu[:, :3 * c]
    mul = jnp.concatenate([mu[:, 3 * c:], jnp.zeros((2, LORA_W - n_lora), F32)], axis=1)
    w2f = jnp.zeros((LORA_W, 2 * c), F32)
    a2f = jnp.zeros((LORA_W, 2 * c), F32)
    for dd in range(2):
        w2f = w2f.at[dd * DECAY_LORA:(dd + 1) * DECAY_LORA, dd * c:(dd + 1) * c].set(decay_w2[li, dd])
        a2f = a2f.at[2 * DECAY_LORA + dd * ICLR_LORA:2 * DECAY_LORA + (dd + 1) * ICLR_LORA,
                     dd * c:(dd + 1) * c].set(iclr_a2[li, dd])
    o_gl = 2 * DECAY_LORA + 2 * ICLR_LORA
    g2f = jnp.zeros((LORA_W, c), F32).at[o_gl:o_gl + GATE_LORA].set(gate_g2[li])
    return dict(
        w_z=w_z, mur=mur, mul=mul,
        w0=decay_w0[li].reshape(1, 2 * c), w2f=w2f.astype(BF16),
        a0=iclr_a0[li].reshape(1, 2 * c), a2f=a2f.astype(BF16), g2f=g2f.astype(BF16),
        kkw=key_kk[li].reshape(1, c), kaw=key_ka[li].reshape(1, c), brk=bonus_rk[li].reshape(1, c),
        gng=gn_g[li].reshape(1, c), gnb=gn_b[li].reshape(1, c),
        gq=jnp.tile(nat_qn_g[li], HEADS).reshape(1, c), gk=jnp.tile(nat_kn_g[li], HEADS).reshape(1, c),
        tab=_nat_bias_table(nat_rpb[li], TOK_TILE),
        pool_w=pool_w[li].astype(BF16), pool_scale=pool_scale[li].reshape(1, c),
        wb=w_branch[li].astype(BF16), wo=w_out[li].astype(BF16),
    )


def kernel(x, c, ctx, c_ctx, w_mod, b_mod, norm_mix_g, norm_ffn_g, w_in, pool_w, pool_scale, shift_mu, decay_w0, decay_w2, iclr_a0, iclr_a2, key_kk, key_ka, bonus_rk, gn_g, gn_b, gate_g2, nat_qn_g, nat_kn_g, nat_rpb, w_branch, w_out, ffn_w1, ffn_w3, ffn_w2, router, moe_w1, moe_w3, moe_w2):
    b, t, d = x.shape
    ctx_len = ctx.shape[1]
    depth = w_mod.shape[0]
    ta = ctx_len + t
    tt = TOK_TILE
    assert ctx_len == tt and t % tt == 0 and t // GRID_W >= NAT_KH and b + 1 <= 8 and CHUNK == HD
    tm = ta // 8
    tm_ffn = ta // 16
    assert ta % 16 == 0 and tm_ffn % 16 == 0
    fc = ffn_w1.shape[-1] // 2

    xa = jnp.concatenate([ctx, x], axis=1)
    cc = jnp.zeros((8, d), F32).at[:b].set(c).at[b].set(c_ctx)
    mod_all = _modulation(cc, w_mod, b_mod)
    head_id = jnp.arange(BRANCH_W) // HD
    bd = (head_id[:, None] == head_id[None, :]).astype(BF16)

    for li in range(depth):
        lw_ = _layer_weights(li, w_in, pool_w, pool_scale, shift_mu, decay_w0, decay_w2, iclr_a0, iclr_a2, key_kk,
                             key_ka, bonus_rk, gn_g, gn_b, gate_g2, nat_qn_g, nat_kn_g, nat_rpb, w_branch, w_out)
        modb = mod_all[li, :b].reshape(b, 6, d)
        modc = mod_all[li, b].reshape(6, d)
        z = _inproj(xa, norm_mix_g[li].reshape(1, d), modc, modb, lw_["w_z"], tt, ctx_len)
        pool = _pool(z, lw_["pool_w"], lw_["pool_scale"], tt, ctx_len)
        r, v, kk, lwd, kd, ka, bonus, gate = _prep(z, lw_["mur"], lw_["mul"], lw_["w0"], lw_["w2f"], lw_["a0"],
                                                   lw_["a2f"], lw_["g2f"], lw_["kkw"], lw_["kaw"], lw_["brk"], bd,
                                                   tt, ctx_len)
        o2 = _wkv(r, v, kk, lwd, kd, ka, ctx_len)
        qn, kn, vb = _qknorm(z, lw_["gq"], lw_["gk"], bd, tt)
        nat = _nat(qn, kn, vb, lw_["tab"], tt, ctx_len)
        xa = _merge(xa, pool, o2, bonus, gate, nat, z, modc, modb, lw_["gng"], lw_["gnb"], bd, lw_["wb"], lw_["wo"],
                    tt, ctx_len)
        j = li // 2
        gf = norm_ffn_g[li].reshape(1, d)
        if li % 2 == 0:
            xa = _ffn(xa, gf, modc, modb, ffn_w1[j].astype(BF16), ffn_w3[j].astype(BF16), ffn_w2[j].astype(BF16),
                      tm_ffn, fc, ctx_len)
        else:
            xa = _moe(xa, gf, modc, modb, router[j], moe_w1[j].astype(BF16), moe_w3[j].astype(BF16),
                      moe_w2[j].astype(BF16), tm, ctx_len, latent_only=li == depth - 1)
    return xa if xa.shape[1] == t else xa[:, ctx_len:]
```

```python
import functools

import jax
import jax.numpy as jnp
from jax import lax
from jax.experimental import pallas as pl
from jax.experimental.pallas import tpu as pltpu

F32 = jnp.float32
BF16 = jnp.bfloat16

GRID_W = 64
BRANCH_W = 512
POOL_WINDOWS = (2, 4, 8, 16)
POOL_GW = 128
HEADS = 8
HD = 64
DECAY_LORA = 32
ICLR_LORA = 32
GATE_LORA = 96
GN_EPS = 64e-5
NAT_KH = 8
NAT_KW = 16
NAT_SCALE = HD ** -0.5
TOP_K = 2
RMS_EPS = 1e-6
NEG = -1e30

ZG_W = 3 * 1024
COL_G = 0
COL_RKV = 3072
COL_Q = 4608
COL_K = 5120
COL_V = 5632
COL_P = 6144
COL_LORA = 6656
Z_W = 6912
LORA_W = 256

TOK_TILE = 256
CHUNK = 64
MOE_TILE = 512
POOL_HALO = 64
V7X_VMEM_LIMIT = 56 * 1024 * 1024


def _bdot(a, b):
    return jnp.dot(a.astype(BF16), b.astype(BF16), preferred_element_type=F32)


def _bdot_nt(a, b):
    return lax.dot_general(a.astype(BF16), b.astype(BF16), (((1,), (1,)), ((), ())), preferred_element_type=F32)


def _split2(x):
    hi = x.astype(BF16)
    lo = (x - hi.astype(F32)).astype(BF16)
    return hi, lo


def _dot2(x, b):
    hi, lo = _split2(x)
    return jnp.dot(hi, b, preferred_element_type=F32) + jnp.dot(lo, b, preferred_element_type=F32)


def _ldot3(a, x):
    h1 = x.astype(BF16)
    r1 = x - h1.astype(F32)
    h2 = r1.astype(BF16)
    h3 = (r1 - h2.astype(F32)).astype(BF16)
    d = lambda h: jnp.dot(a, h, preferred_element_type=F32)
    return d(h1) + d(h2) + d(h3)


def _silu(x):
    return x * jax.nn.sigmoid(x)


def _params(sem, vmem=None, **kw):
    return pltpu.CompilerParams(dimension_semantics=sem, vmem_limit_bytes=vmem, **kw)


def _mod_kernel(c_ref, w_ref, b_ref, o_ref):
    o_ref[...] = jnp.dot(_silu(c_ref[...]), w_ref[...], precision=lax.Precision.HIGHEST,
                         preferred_element_type=F32) + b_ref[...]


def _modulation(cc, w_mod, b_mod):
    depth, d, n = w_mod.shape
    tn = 1536
    return pl.pallas_call(
        _mod_kernel,
        grid=(depth, n // tn),
        in_specs=[pl.BlockSpec((8, d), lambda l, j: (0, 0)),
                  pl.BlockSpec((None, d, tn), lambda l, j: (l, 0, j)),
                  pl.BlockSpec((None, 1, tn), lambda l, j: (l, 0, j))],
        out_specs=pl.BlockSpec((None, 8, tn), lambda l, j: (l, 0, j)),
        out_shape=jax.ShapeDtypeStruct((depth, 8, n), F32),
        compiler_params=_params(("arbitrary", "arbitrary"), 40 * 1024 * 1024),
        name="modulation",
    )(cc, w_mod, b_mod.reshape(depth, 1, n))


def _norm_mod(x, g, modc_ref, modb_ref, row0, ctx_len, k_shift, k_scale):
    rows = x.shape[0]
    isctx = (row0 + lax.broadcasted_iota(jnp.int32, (rows, 1), 0)) < ctx_len
    sh = jnp.where(isctx, modc_ref[k_shift:k_shift + 1, :], modb_ref[k_shift:k_shift + 1, :])
    sc = jnp.where(isctx, modc_ref[k_scale:k_scale + 1, :], modb_ref[k_scale:k_scale + 1, :])
    ms = jnp.mean(x * x, axis=-1, keepdims=True)
    return (x * lax.rsqrt(ms + RMS_EPS) * g) * (1.0 + sc) + sh


def _gate_rows(modc_ref, modb_ref, row0, rows, ctx_len, k_gate):
    isctx = (row0 + lax.broadcasted_iota(jnp.int32, (rows, 1), 0)) < ctx_len
    return jnp.where(isctx, modc_ref[k_gate:k_gate + 1, :], modb_ref[k_gate:k_gate + 1, :])


def _inproj_kernel(x_ref, g_ref, modc_ref, modb_ref, w_ref, z_ref, *, tt, ctx_len):
    i = pl.program_id(2)
    h = _norm_mod(x_ref[...], g_ref[...], modc_ref, modb_ref, i * tt, ctx_len, 0, 1)
    z_ref[...] = _bdot(h, w_ref[...])


def _inproj(xa, g, modc, modb, w, tt, ctx_len):
    b, ta, d = xa.shape
    nh = 2
    tn = Z_W // nh
    return pl.pallas_call(
        functools.partial(_inproj_kernel, tt=tt, ctx_len=ctx_len),
        grid=(nh, b, ta // tt),
        in_specs=[pl.BlockSpec((None, tt, d), lambda n, bb, i: (bb, i, 0)),
                  pl.BlockSpec((1, d), lambda n, bb, i: (0, 0)),
                  pl.BlockSpec((6, d), lambda n, bb, i: (0, 0)),
                  pl.BlockSpec((None, 6, d), lambda n, bb, i: (bb, 0, 0)),
                  pl.BlockSpec((d, tn), lambda n, bb, i: (0, n))],
        out_specs=pl.BlockSpec((None, tt, tn), lambda n, bb, i: (bb, i, n)),
        out_shape=jax.ShapeDtypeStruct((b, ta, Z_W), F32),
        compiler_params=_params(("arbitrary",) * 3, 48 * 1024 * 1024),
        name="inproj",
    )(xa, g, modc, modb, w)


def _pool_kernel(zp_ref, zc_ref, zn_ref, pw_ref, ps_ref, o_ref, *, tt, ctx_len, ta):
    i = pl.program_id(1)
    zc = zc_ref[...]
    zcat = jnp.concatenate([zp_ref[...], zc, zn_ref[...]], axis=0)
    t_g = i * tt + lax.broadcasted_iota(jnp.int32, (tt, 1), 0)
    s_g = i * tt - POOL_HALO + lax.broadcasted_iota(jnp.int32, (tt, tt + 2 * POOL_HALO), 1)
    in_ctx = i * tt < ctx_len
    seg_lo = jnp.where(in_ctx, 0, ctx_len)
    seg_hi = jnp.where(in_ctx, ctx_len, ta)
    for gi, win in enumerate(POOL_WINDOWS):
        sl = slice(gi * POOL_GW, (gi + 1) * POOL_GW)
        lo = jnp.clip(t_g - win // 2, seg_lo, seg_hi)
        hi = jnp.clip(t_g - win // 2 + win, seg_lo, seg_hi)
        msk = jnp.where(s_g >= lo, jnp.where(s_g < hi, 1.0, 0.0), 0.0).astype(BF16)
        hi_b, lo_b = _split2(zcat[:, sl])
        wsum = jnp.dot(msk, hi_b, preferred_element_type=F32) + jnp.dot(msk, lo_b, preferred_element_type=F32)
        p = wsum / (hi - lo).astype(F32) - zc[:, sl]
        o_ref[:, sl] = _bdot(p, pw_ref[gi]) * ps_ref[:, sl]


def _pool(z, pool_w, pool_scale, tt, ctx_len):
    b, ta, _ = z.shape
    halo = POOL_HALO
    nb = ta // halo
    cb = COL_P // BRANCH_W
    return pl.pallas_call(
        functools.partial(_pool_kernel, tt=tt, ctx_len=ctx_len, ta=ta),
        grid=(b, ta // tt),
        in_specs=[pl.BlockSpec((None, halo, BRANCH_W), lambda bb, i: (bb, jnp.maximum(i * (tt // halo) - 1, 0), cb)),
                  pl.BlockSpec((None, tt, BRANCH_W), lambda bb, i: (bb, i, cb)),
                  pl.BlockSpec((None, halo, BRANCH_W), lambda bb, i: (bb, jnp.minimum((i + 1) * (tt // halo), nb - 1), cb)),
                  pl.BlockSpec((4, POOL_GW, POOL_GW), lambda bb, i: (0, 0, 0)),
                  pl.BlockSpec((1, BRANCH_W), lambda bb, i: (0, 0))],
        out_specs=pl.BlockSpec((None, tt, BRANCH_W), lambda bb, i: (bb, i, 0)),
        out_shape=jax.ShapeDtypeStruct((b, ta, BRANCH_W), F32),
        compiler_params=_params(("arbitrary",) * 2),
        name="pool",
    )(z, z, z, pool_w, pool_scale)


def _shifted(z, zprev, znext, mu, first, last):
    tt = z.shape[0]
    ridx = lax.broadcasted_iota(jnp.int32, (tt, 1), 0)
    zp = pltpu.roll(z, 1, 0)
    zp = jnp.where(ridx == 0, zprev[7:8, :], zp)
    zp = jnp.where(first, 0.0, zp)
    zn = pltpu.roll(z, tt - 1, 0)
    zn = jnp.where(ridx == tt - 1, znext[0:1, :], zn)
    zn = jnp.where(last, 0.0, zn)
    return z + mu[0:1, :] * (zp - z) + mu[1:2, :] * (zn - z)


def _prep_kernel(rp_ref, rc_ref, rn_ref, lp_ref, lc_ref, ln_ref, mur_ref, mul_ref, w0_ref, w2_ref, a0_ref, a2_ref,
                 g2_ref, kkw_ref, kaw_ref, brk_ref, bd_ref,
                 r_ref, v_ref, kk_ref, lw_ref, kd_ref, ka_ref, bon_ref, gate_ref, *, tt, ctx_len, ta):
    i = pl.program_id(1)
    grow = i * tt + lax.broadcasted_iota(jnp.int32, (tt, 1), 0)
    first = (grow == 0) | (grow == ctx_len)
    last = (grow == ctx_len - 1) | (grow == ta - 1)
    rkv = _shifted(rc_ref[...], rp_ref[...], rn_ref[...], mur_ref[...], first, last)
    lo = _shifted(lc_ref[...], lp_ref[...], ln_ref[...], mul_ref[...], first, last)
    c = BRANCH_W
    r, k, v = rkv[:, :c], rkv[:, c:2 * c], rkv[:, 2 * c:]
    y = _bdot(jnp.tanh(lo), w2_ref[...]) + w0_ref[...]
    lw = -jnp.exp(-0.5) * jax.nn.sigmoid(y)
    a = jax.nn.sigmoid(_bdot(lo, a2_ref[...]) + a0_ref[...])
    gate_ref[...] = _bdot(jax.nn.sigmoid(lo), g2_ref[...])
    bd = bd_ref[...]
    kk = k * kkw_ref[...]
    ss = _dot2(kk * kk, bd)
    kk = kk / jnp.maximum(jnp.sqrt(ss), 1e-12)
    a_f, a_b = a[:, :c], a[:, c:]
    kaw = kaw_ref[...]
    kd_f = k * (1.0 + (a_f - 1.0) * kaw)
    kd_b = k * (1.0 + (a_b - 1.0) * kaw)
    r_ref[...] = r
    v_ref[...] = v.astype(BF16)
    kk_ref[...] = kk
    lw_ref[...] = lw
    kd_ref[:, :c] = kd_f
    kd_ref[:, c:] = kd_b
    ka_ref[:, :c] = a_f * kk
    ka_ref[:, c:] = a_b * kk
    bon_ref[...] = _dot2(r * (kd_f + kd_b) * brk_ref[...], bd) * v


def _prep(z, mur, mul, w0, w2f, a0, a2f, g2f, kkw, kaw, brk, bd, tt, ctx_len):
    b, ta, _ = z.shape
    nb = ta // 8
    c = BRANCH_W
    rb, lb = COL_RKV // (3 * c), COL_LORA // LORA_W
    prev = lambda bb, i: (bb, jnp.maximum(i * (tt // 8) - 1, 0))
    nxt = lambda bb, i: (bb, jnp.minimum((i + 1) * (tt // 8), nb - 1))
    full = lambda shape: pl.BlockSpec(shape, lambda bb, i: (0,) * len(shape))
    tok = lambda w: pl.BlockSpec((None, tt, w), lambda bb, i: (bb, i, 0))
    sds = lambda w: jax.ShapeDtypeStruct((b, ta, w), F32)
    return pl.pallas_call(
        functools.partial(_prep_kernel, tt=tt, ctx_len=ctx_len, ta=ta),
        grid=(b, ta // tt),
        in_specs=[pl.BlockSpec((None, 8, 3 * c), lambda bb, i: prev(bb, i) + (rb,)),
                  pl.BlockSpec((None, tt, 3 * c), lambda bb, i: (bb, i, rb)),
                  pl.BlockSpec((None, 8, 3 * c), lambda bb, i: nxt(bb, i) + (rb,)),
                  pl.BlockSpec((None, 8, LORA_W), lambda bb, i: prev(bb, i) + (lb,)),
                  pl.BlockSpec((None, tt, LORA_W), lambda bb, i: (bb, i, lb)),
                  pl.BlockSpec((None, 8, LORA_W), lambda bb, i: nxt(bb, i) + (lb,)),
                  full((2, 3 * c)), full((2, LORA_W)), full((1, 2 * c)), full((LORA_W, 2 * c)), full((1, 2 * c)),
                  full((LORA_W, 2 * c)), full((LORA_W, c)), full((1, c)), full((1, c)), full((1, c)), full((c, c))],
        out_specs=[tok(c), tok(c), tok(c), tok(2 * c), tok(2 * c), tok(2 * c), tok(c), tok(c)],
        out_shape=[sds(c), jax.ShapeDtypeStruct((b, ta, c), BF16), sds(c), sds(2 * c), sds(2 * c), sds(2 * c),
                   sds(c), sds(c)],
        compiler_params=_params(("arbitrary",) * 2, 40 * 1024 * 1024),
        name="rwkv_prep",
    )(z, z, z, z, z, z, mur, mul, w0, w2f, a0, a2f, g2f, kkw, kaw, brk, bd)


def _wkv_pre_kernel(r_ref, v_ref, kk_ref, lw_ref, kd_ref, ka_ref,
                    kqt_ref, rqt_ref, u0_ref, o0_ref, xt_ref, pt_ref, *, g_chunks):
    d = pl.program_id(1)
    C = CHUNK
    n = g_chunks * C
    sgn = 1 - 2 * d
    row = lax.broadcasted_iota(jnp.int32, (n, n), 0)
    col = lax.broadcasted_iota(jnp.int32, (n, n), 1)
    same = (row // C) == (col // C)
    tri = jnp.where(same, jnp.where((col - row) * sgn <= 0, 1.0, 0.0), 0.0).astype(BF16)
    lw = lw_ref[...]
    cs = _ldot3(tri, lw)
    kd = kd_ref[...]
    ka = ka_ref[...]
    kq_all = kk_ref[...] * jnp.exp(cs - lw)
    rq_all = r_ref[...] * jnp.exp(cs)
    p_inv = jnp.exp(-cs)
    kdi_all = kd * p_inv
    kai_all = ka * p_inv
    v_all = v_ref[...]

    t_i = lax.broadcasted_iota(jnp.int32, (C, 2 * C), 0)
    s_i = lax.broadcasted_iota(jnp.int32, (C, 2 * C), 1) & (C - 1)
    strict = (s_i - t_i) * sgn < 0
    incl = (s_i - t_i) * sgn <= 0
    lane = lax.broadcasted_iota(jnp.int32, (1, 2 * HD), 1)
    m1 = jnp.where(lane < HD, 1.0, 0.0)
    m2 = 1.0 - m1
    r2 = lax.broadcasted_iota(jnp.int32, (2 * HD, 2 * HD), 0)
    c2 = lax.broadcasted_iota(jnp.int32, (2 * HD, 2 * HD), 1)
    eye = jnp.where(r2 == c2, 1.0, 0.0)
    stack2 = lambda x: jnp.concatenate([x * m1, x * m2], axis=0)

    for gc in range(g_chunks):
        rows = slice(gc * C, (gc + 1) * C)
        cs_c = cs[rows]
        tot = jnp.where(d == 1, cs_c[0:1, :], cs_c[C - 1:C, :])
        p_end = jnp.exp(tot - cs_c)
        pt_ref[gc] = jnp.broadcast_to(jnp.exp(tot), (8, tot.shape[-1]))
        kde = kd[rows] * p_end
        kae = ka[rows] * p_end
        for p in range(HEADS // 2):
            sl = slice(2 * HD * p, 2 * HD * (p + 1))
            xt_ref[gc, p] = jnp.concatenate([kde[:, sl], -kae[:, sl]], axis=0).T.astype(BF16)

    units = [(slice(gc * C, (gc + 1) * C), slice(2 * HD * p, 2 * HD * (p + 1)))
             for gc in range(g_chunks) for p in range(HEADS // 2)]
    each = lambda f, *lists: [f(*a) for a in zip(*lists)]
    kq = [kq_all[rows, sl] for rows, sl in units]
    rq = [rq_all[rows, sl] for rows, sl in units]
    g = [_bdot_nt(jnp.concatenate([kq[n_], rq[n_]], axis=0),
                  jnp.concatenate([stack2(kdi_all[rows, sl]), stack2(kai_all[rows, sl])], axis=0))
         for n_, (rows, sl) in enumerate(units)]
    akk = [jnp.where(strict, x[:C, :2 * C], 0.0) for x in g]
    ark = [jnp.where(incl, x[C:, :2 * C], 0.0) for x in g]
    ara = [jnp.where(incl, x[C:, 2 * C:], 0.0) for x in g]
    lmat = [stack2(jnp.where(strict, x[:C, 2 * C:], 0.0)) for x in g]
    tm = [eye - x for x in lmat]
    pw = each(_bdot, lmat, lmat)
    n_sq = CHUNK.bit_length() - 2
    for it in range(n_sq):
        tm = each(lambda t_, p_: t_ + _bdot(t_, p_), tm, pw)
        if it + 1 < n_sq:
            pw = each(_bdot, pw, pw)
    tss = [x[:C, :] + x[C:, :] for x in tm]
    vst = [stack2(v_all[rows, sl].astype(F32)) for rows, sl in units]
    kqt = each(lambda t_, k_: _bdot(t_, stack2(k_)), tss, kq)
    akv = each(_bdot, akk, vst)
    u0 = each(lambda t_, a_: _bdot(t_, stack2(a_)), tss, akv)
    rqt = each(lambda r_, a_, k_: r_ - _bdot(a_, stack2(k_)), rq, ara, kqt)
    o0 = each(lambda ak_, aa_, v_, u_: _bdot(jnp.concatenate([ak_, -aa_], axis=1),
                                             jnp.concatenate([v_, stack2(u_)], axis=0)), ark, ara, vst, u0)
    for n_, (rows, sl) in enumerate(units):
        kqt_ref[rows, sl] = kqt[n_].astype(BF16)
        rqt_ref[rows, sl] = rqt[n_].astype(BF16)
        u0_ref[rows, sl] = u0[n_]
        o0_ref[rows, sl] = o0[n_]


def _wkv_seq_kernel(*refs, n_batch, cps):
    ins, (of_ref, ob_ref), st_ref = refs[:14], refs[14:16], refs[16]
    ci = pl.program_id(0)
    C = CHUNK

    @pl.when(ci == 0)
    def _():
        st_ref[...] = jnp.zeros_like(st_ref)

    r2 = lax.broadcasted_iota(jnp.int32, (2 * HD, 2 * HD), 0)
    c2 = lax.broadcasted_iota(jnp.int32, (2 * HD, 2 * HD), 1)
    bdm = (r2 < HD) == (c2 < HD)
    units = [(d, bb, p) for d in range(2) for bb in range(n_batch) for p in range(HEADS // 2)]
    st = [st_ref[d, bb, p] for d, bb, p in units]
    for q_ in range(cps):
        uo, u = [], []
        for n_, (d, bb, p) in enumerate(units):
            kqt_ref, rqt_ref = ins[7 * d], ins[7 * d + 1]
            q = q_ if d == 0 else cps - 1 - q_
            rows, sl = slice(q * C, (q + 1) * C), slice(2 * HD * p, 2 * HD * (p + 1))
            lhs = jnp.concatenate([kqt_ref[bb, rows, sl], rqt_ref[bb, rows, sl]], axis=0)
            uo.append(jnp.dot(lhs, st[n_].astype(BF16), preferred_element_type=F32))
        for n_, (d, bb, p) in enumerate(units):
            u0_ref, o0_ref = ins[7 * d + 2], ins[7 * d + 3]
            o_ref = of_ref if d == 0 else ob_ref
            q = q_ if d == 0 else cps - 1 - q_
            rows, sl = slice(q * C, (q + 1) * C), slice(2 * HD * p, 2 * HD * (p + 1))
            u.append(uo[n_][:C, :] + u0_ref[bb, rows, sl])
            o_ref[bb, rows, sl] = uo[n_][C:, :] + o0_ref[bb, rows, sl]
        for n_, (d, bb, p) in enumerate(units):
            xt_ref, pt_ref, v_ref = ins[7 * d + 4], ins[7 * d + 5], ins[7 * d + 6]
            q = q_ if d == 0 else cps - 1 - q_
            rows, sl = slice(q * C, (q + 1) * C), slice(2 * HD * p, 2 * HD * (p + 1))
            yv = jnp.concatenate([v_ref[bb, rows, sl], u[n_].astype(BF16)], axis=0)
            upd = jnp.dot(xt_ref[bb, q, p], yv, preferred_element_type=F32)
            pcol = jnp.broadcast_to(pt_ref[bb, q, 0:1, sl], (2 * HD, 2 * HD)).T
            st[n_] = st[n_] * pcol + jnp.where(bdm, upd, 0.0)
    for n_, (d, bb, p) in enumerate(units):
        st_ref[d, bb, p] = st[n_]


def _wkv(r, v, kk, lw, kd, ka, ctx_len):
    b, ta, c = r.shape
    nch = ta // CHUNK
    ncc = ctx_len // CHUNK
    gch = 4
    assert nch % gch == 0
    n = gch * CHUNK
    npair = HEADS // 2
    shared = pl.BlockSpec((None, n, c), lambda bb, d, i: (bb, i, 0))
    per_dir = pl.BlockSpec((None, n, c), lambda bb, d, i: (bb, i, d))
    tok = pl.BlockSpec((None, None, n, c), lambda bb, d, i: (d, bb, i, 0))
    kqt, rqt, u0, o0, xt, pt = pl.pallas_call(
        functools.partial(_wkv_pre_kernel, g_chunks=gch),
        grid=(b, 2, nch // gch),
        in_specs=[shared, shared, shared, per_dir, per_dir, per_dir],
        out_specs=[tok, tok, tok, tok,
                   pl.BlockSpec((None, None, gch, npair, 2 * HD, 2 * CHUNK), lambda bb, d, i: (d, bb, i, 0, 0, 0)),
                   pl.BlockSpec((None, None, gch, 8, c), lambda bb, d, i: (d, bb, i, 0, 0))],
        out_shape=[jax.ShapeDtypeStruct((2, b, ta, c), BF16), jax.ShapeDtypeStruct((2, b, ta, c), BF16),
                   jax.ShapeDtypeStruct((2, b, ta, c), F32), jax.ShapeDtypeStruct((2, b, ta, c), F32),
                   jax.ShapeDtypeStruct((2, b, nch, npair, 2 * HD, 2 * CHUNK), BF16),
                   jax.ShapeDtypeStruct((2, b, nch, 8, c), F32)],
        compiler_params=_params(("arbitrary",) * 3, 40 * 1024 * 1024),
        name="wkv_pre",
    )(r, v, kk, lw, kd, ka)

    cps = 2
    assert nch % cps == 0 and ncc % cps == 0
    nsb, ncb = nch // cps, ncc // cps

    def blk(d, ci):
        return ci if d == 0 else jnp.where(ci < ncb, ncb - 1 - ci, nsb + ncb - 1 - ci)

    in_specs, args = [], []
    for d in range(2):
        tokd = pl.BlockSpec((None, b, cps * CHUNK, c), lambda ci, d=d: (d, 0, blk(d, ci), 0))
        in_specs += [tokd, tokd, tokd, tokd,
                     pl.BlockSpec((None, b, cps, npair, 2 * HD, 2 * CHUNK),
                                  lambda ci, d=d: (d, 0, blk(d, ci), 0, 0, 0)),
                     pl.BlockSpec((None, b, cps, 8, c), lambda ci, d=d: (d, 0, blk(d, ci), 0, 0)),
                     pl.BlockSpec((b, cps * CHUNK, c), lambda ci, d=d: (0, blk(d, ci), 0))]
        args += [kqt, rqt, u0, o0, xt, pt, v]
    return pl.pallas_call(
        functools.partial(_wkv_seq_kernel, n_batch=b, cps=cps),
        grid=(nsb,),
        in_specs=in_specs,
        out_specs=[pl.BlockSpec((b, cps * CHUNK, c), lambda ci, d=d: (0, blk(d, ci), 0)) for d in range(2)],
        out_shape=[jax.ShapeDtypeStruct((b, ta, c), F32)] * 2,
        scratch_shapes=[pltpu.VMEM((2, b, npair, 2 * HD, 2 * HD), F32)],
        compiler_params=_params(("arbitrary",)),
        name="wkv_seq",
    )(*args)


def _qknorm_kernel(q_ref, k_ref, v_ref, gq_ref, gk_ref, bd_ref, qo_ref, ko_ref, vo_ref):
    bd = bd_ref[...]

    def norm(x, g):
        ms = _dot2(x * x, bd) * (1.0 / HD)
        return x * lax.rsqrt(ms + RMS_EPS) * g

    qo_ref[...] = (norm(q_ref[...], gq_ref[...]) * NAT_SCALE).astype(BF16)
    ko_ref[...] = norm(k_ref[...], gk_ref[...]).astype(BF16)
    vo_ref[...] = v_ref[...].astype(BF16)


def _qknorm(z, gq, gk, bd, tt):
    b, ta, _ = z.shape
    c = BRANCH_W
    col = lambda cb: pl.BlockSpec((None, tt, c), lambda bb, i: (bb, i, cb))
    full = lambda shape: pl.BlockSpec(shape, lambda bb, i: (0,) * len(shape))
    out = pl.BlockSpec((None, tt, c), lambda bb, i: (bb, i, 0))
    sds = jax.ShapeDtypeStruct((b, ta, c), BF16)
    return pl.pallas_call(
        _qknorm_kernel,
        grid=(b, ta // tt),
        in_specs=[col(COL_Q // c), col(COL_K // c), col(COL_V // c), full((1, c)), full((1, c)), full((c, c))],
        out_specs=[out, out, out],
        out_shape=[sds, sds, sds],
        compiler_params=_params(("arbitrary",) * 2),
        name="nat_qknorm",
    )(z, z, z, gq, gk, bd)


def _nat_kernel(q_ref, kp_ref, kc_ref, kn_ref, vp_ref, vc_ref, vn_ref, kx_ref, vx_ref, bias_ref, o_ref,
                kbuf, vbuf, *, tt):
    i = pl.program_id(1)
    lane = lax.broadcasted_iota(jnp.int32, (1, 2 * HD), 1)
    first = lane < HD

    def softmax_pv(parts):
        mx = None
        for s, _ in parts:
            m = jnp.max(s, axis=-1, keepdims=True)
            mx = m if mx is None else jnp.maximum(mx, m)
        den = 0.0
        acc = 0.0
        for s, vv in parts:
            e = jnp.exp(s - mx)
            den = den + jnp.sum(e, axis=-1, keepdims=True)
            acc = acc + _bdot(e, vv)
        return acc / den

    @pl.when(i == 0)
    def _():
        for p in range(HEADS // 2):
            sl = slice(2 * HD * p, 2 * HD * (p + 1))
            qp = q_ref[:, sl]
            kx = kx_ref[:, sl]
            vx = vx_ref[:, sl]
            outs = []
            for h in range(2):
                qm = jnp.where(first if h == 0 else ~first, qp, jnp.zeros_like(qp))
                outs.append(softmax_pv([(_bdot_nt(qm, kx), vx)]))
            o_ref[:, sl] = jnp.where(first, outs[0], outs[1])

    @pl.when(i > 0)
    def _():
        for n_, (kr, vr) in enumerate(((kp_ref, vp_ref), (kc_ref, vc_ref), (kn_ref, vn_ref), (kx_ref, vx_ref))):
            kbuf[n_ * tt:(n_ + 1) * tt, :] = kr[...]
            vbuf[n_ * tt:(n_ + 1) * tt, :] = vr[...]

        def scores(hd):
            sl = slice(2 * HD * (hd // 2), 2 * HD * (hd // 2 + 1))
            qp = q_ref[:, sl]
            qm = jnp.where(first if hd % 2 == 0 else ~first, qp, jnp.zeros_like(qp))
            return _bdot_nt(qm, kbuf[:, sl])

        s_next = scores(0)
        outs = []
        for hd in range(HEADS):
            s = s_next
            if hd + 1 < HEADS:
                s_next = scores(hd + 1)
            sl = slice(2 * HD * (hd // 2), 2 * HD * (hd // 2 + 1))
            s_loc = s[:, :3 * tt] + bias_ref[hd]
            s_ctx = s[:, 3 * tt:]
            mx = jnp.maximum(jnp.max(s_loc, axis=-1, keepdims=True), jnp.max(s_ctx, axis=-1, keepdims=True))
            e = jnp.concatenate([jnp.exp(s_loc - mx), jnp.exp(s_ctx - mx)], axis=1)
            den = jnp.sum(e, axis=-1, keepdims=True)
            outs.append(_bdot(e, vbuf[:, sl]) / den)
            if hd % 2 == 1:
                o_ref[:, sl] = jnp.where(first, outs[hd - 1], outs[hd])


def _nat(qn, kn, vb, bias3, tt, ctx_len):
    b, ta, c = qn.shape
    nt = ta // tt
    cur = pl.BlockSpec((None, tt, c), lambda bb, i: (bb, i, 0))
    prv = pl.BlockSpec((None, tt, c), lambda bb, i: (bb, jnp.maximum(i - 1, 0), 0))
    nxt = pl.BlockSpec((None, tt, c), lambda bb, i: (bb, jnp.minimum(i + 1, nt - 1), 0))
    cx = pl.BlockSpec((None, tt, c), lambda bb, i: (bb, 0, 0))
    variant = lambda i: jnp.where(i <= 1, 0, jnp.where(i == nt - 1, 2, 1))
    return pl.pallas_call(
        functools.partial(_nat_kernel, tt=tt),
        grid=(b, nt),
        in_specs=[cur, prv, cur, nxt, prv, cur, nxt, cx, cx,
                  pl.BlockSpec((None, HEADS, tt, 3 * tt), lambda bb, i: (variant(i), 0, 0, 0))],
        out_specs=pl.BlockSpec((None, tt, c), lambda bb, i: (bb, i, 0)),
        out_shape=jax.ShapeDtypeStruct((b, ta, c), F32),
        scratch_shapes=[pltpu.VMEM((4 * tt, c), BF16), pltpu.VMEM((4 * tt, c), BF16)],
        compiler_params=_params(("arbitrary",) * 2, 48 * 1024 * 1024),
        name="nat",
    )(qn, kn, kn, kn, vb, vb, vb, kn, vb, bias3)


def _nat_bias_table(rpb, tt):
    h = rpb.shape[0]
    rpt = tt // GRID_W
    cols = jnp.arange(GRID_W)
    cstart = jnp.clip(cols - NAT_KW // 2, 0, GRID_W - NAT_KW)
    kc = cols[None, :]
    col_ok = (kc >= cstart[:, None]) & (kc < cstart[:, None] + NAT_KW)
    idx = jnp.clip(kc - cols[:, None] + (NAT_KW - 1), 0, 2 * NAT_KW - 2)
    t15 = jnp.where(col_ok[None, None], rpb[:, :, idx], NEG).astype(F32)
    a = jnp.arange(rpt)[:, None]
    j = jnp.arange(3 * rpt)[None, :]
    d = j - a - rpt + (NAT_KH - 1)
    blocks = t15[:, d]
    lo = jnp.stack([jnp.full((rpt,), rpt), a[:, 0] + rpt - NAT_KH // 2, jnp.full((rpt,), 2 * rpt - NAT_KH)])
    row_ok = (j[None] >= lo[:, :, None]) & (j[None] < lo[:, :, None] + NAT_KH)
    full = jnp.where(row_ok[:, None, :, :, None, None], blocks[None], NEG)
    return full.transpose(0, 1, 2, 4, 3, 5).reshape(3, h, tt, 3 * tt)


def _merge_kernel(x_ref, pool_ref, of_ref, ob_ref, bon_ref, gate_ref, nat_ref, zg_ref, modc_ref, modb_ref,
                  gng_ref, gnb_ref, bd_ref, wb_ref, wo_ref, o_ref, *, tt, ctx_len):
    i = pl.program_id(1)
    bd = bd_ref[...]
    wkv = of_ref[...] + ob_ref[...]
    mu = _dot2(wkv, bd) * (1.0 / HD)
    cen = wkv - mu
    var = _dot2(cen * cen, bd) * (1.0 / HD)
    y = cen * lax.rsqrt(var + GN_EPS) * gng_ref[...] + gnb_ref[...]
    rw = (y + bon_ref[...]) * gate_ref[...]
    d = x_ref.shape[-1]
    m = jnp.zeros((tt, d), F32)
    for n, br in enumerate((pool_ref[...], rw, nat_ref[...])):
        m = m + jax.nn.sigmoid(zg_ref[:, n * d:(n + 1) * d]) * _bdot(br, wb_ref[n])
    yout = _bdot(m, wo_ref[...])
    o_ref[...] = x_ref[...] + _gate_rows(modc_ref, modb_ref, i * tt, tt, ctx_len, 2) * yout


def _merge(xa, pool, o2, bonus, gate, nat, z, modc, modb, gng, gnb, bd, wb, wo, tt, ctx_len):
    b, ta, d = xa.shape
    c = BRANCH_W
    tok = lambda w: pl.BlockSpec((None, tt, w), lambda bb, i: (bb, i, 0))
    full = lambda shape: pl.BlockSpec(shape, lambda bb, i: (0,) * len(shape))
    return pl.pallas_call(
        functools.partial(_merge_kernel, tt=tt, ctx_len=ctx_len),
        grid=(b, ta // tt),
        in_specs=[tok(d), tok(c), tok(c), tok(c), tok(c), tok(c), tok(c),
                  pl.BlockSpec((None, tt, ZG_W), lambda bb, i: (bb, i, COL_G // ZG_W)),
                  full((6, d)), pl.BlockSpec((None, 6, d), lambda bb, i: (bb, 0, 0)),
                  full((1, c)), full((1, c)), full((c, c)), full((3, c, d)), full((d, d))],
        out_specs=tok(d),
        out_shape=jax.ShapeDtypeStruct((b, ta, d), F32),
        compiler_params=_params(("arbitrary",) * 2, 48 * 1024 * 1024),
        name="merge",
    )(xa, pool, o2[0], o2[1], bonus, gate, nat, z, modc, modb, gng, gnb, bd, wb, wo)


def _swiglu_halves(w1, w3, w2):
    *lead, d, dff = w1.shape
    fc = dff // 2
    half = lambda w: jnp.moveaxis(w.reshape(*lead, d, 2, fc), -2, -3)
    w13 = jnp.concatenate([half(w1), half(w3)], axis=-1).astype(BF16)
    return w13, w2.reshape(*lead, 2, fc, d).astype(BF16)


def _ffn_kernel(x_ref, g_ref, modc_ref, modb_ref, w13_ref, w2_ref, o_ref, h_scr, acc, *, tm, ctx_len):
    i = pl.program_id(1)
    f = pl.program_id(2)
    fc = w2_ref.shape[0]

    @pl.when(f == 0)
    def _():
        h_scr[...] = _norm_mod(x_ref[...], g_ref[...], modc_ref, modb_ref, i * tm, ctx_len, 3, 4).astype(BF16)
        acc[...] = jnp.zeros_like(acc)

    h = h_scr[...]
    a = jnp.dot(h, w13_ref[:, :fc], preferred_element_type=F32)
    bb = jnp.dot(h, w13_ref[:, fc:], preferred_element_type=F32)
    acc[...] += _bdot(_silu(a) * bb, w2_ref[...])

    @pl.when(f == pl.num_programs(2) - 1)
    def _():
        o_ref[...] = x_ref[...] + _gate_rows(modc_ref, modb_ref, i * tm, tm, ctx_len, 5) * acc[...]


def _ffn(xa, g, modc, modb, w13, w2, tm, ctx_len):
    b, ta, d = xa.shape
    nf, fc, _ = w2.shape
    tok = pl.BlockSpec((None, tm, d), lambda bb, i, f: (bb, i, 0))
    return pl.pallas_call(
        functools.partial(_ffn_kernel, tm=tm, ctx_len=ctx_len),
        grid=(b, ta // tm, nf),
        in_specs=[tok,
                  pl.BlockSpec((1, d), lambda bb, i, f: (0, 0)),
                  pl.BlockSpec((6, d), lambda bb, i, f: (0, 0)),
                  pl.BlockSpec((None, 6, d), lambda bb, i, f: (bb, 0, 0)),
                  pl.BlockSpec((None, d, 2 * fc), lambda bb, i, f: (f, 0, 0)),
                  pl.BlockSpec((None, fc, d), lambda bb, i, f: (f, 0, 0))],
        out_specs=tok,
        out_shape=jax.ShapeDtypeStruct((b, ta, d), F32),
        scratch_shapes=[pltpu.VMEM((tm, d), BF16), pltpu.VMEM((tm, d), F32)],
        compiler_params=_params(("arbitrary",) * 3, 48 * 1024 * 1024),
        name="ffn",
    )(xa, g, modc, modb, w13, w2)


def _route_kernel(x_ref, g_ref, modc_ref, modb_ref, rt_ref, h_ref, r_ref, *, tm, ctx_len, n_exp):
    i = pl.program_id(1)
    lane = lax.broadcasted_iota(jnp.int32, (tm, 128), 1)
    h = _norm_mod(x_ref[...], g_ref[...], modc_ref, modb_ref, i * tm, ctx_len, 3, 4)
    h_ref[...] = h
    logits = jnp.dot(h, rt_ref[...], precision=lax.Precision.HIGHEST, preferred_element_type=F32)
    logits = jnp.where(lane < n_exp, logits, NEG)
    m1 = jnp.max(logits, axis=-1, keepdims=True)
    i1 = jnp.min(jnp.where(logits == m1, lane, 128), axis=-1, keepdims=True)
    rest = jnp.where(lane == i1, NEG, logits)
    m2 = jnp.max(rest, axis=-1, keepdims=True)
    i2 = jnp.min(jnp.where(rest == m2, lane, 128), axis=-1, keepdims=True)
    e2 = jnp.exp(m2 - m1)
    den = 1.0 + e2
    r_ref[...] = jnp.where(lane == 0, i1.astype(F32),
                           jnp.where(lane == 1, i2.astype(F32), jnp.where(lane == 2, 1.0 / den, e2 / den)))


def _route(xa, g, modc, modb, router, tm, ctx_len, n_exp):
    b, ta, d = xa.shape
    tok = lambda w: pl.BlockSpec((None, tm, w), lambda bb, i: (bb, i, 0))
    return pl.pallas_call(
        functools.partial(_route_kernel, tm=tm, ctx_len=ctx_len, n_exp=n_exp),
        grid=(b, ta // tm),
        in_specs=[tok(d),
                  pl.BlockSpec((1, d), lambda bb, i: (0, 0)),
                  pl.BlockSpec((6, d), lambda bb, i: (0, 0)),
                  pl.BlockSpec((None, 6, d), lambda bb, i: (bb, 0, 0)),
                  pl.BlockSpec((d, 128), lambda bb, i: (0, 0))],
        out_specs=[tok(d), tok(128)],
        out_shape=[jax.ShapeDtypeStruct((b, ta, d), F32), jax.ShapeDtypeStruct((b, ta, 128), F32)],
        compiler_params=_params(("arbitrary",) * 2, 40 * 1024 * 1024),
        name="moe_route",
    )(xa, g, modc, modb, router)


def _experts_kernel(te_ref, nact_ref, src_ref, dst_ref, h_hbm, wrow_ref, w13_ref, w2_ref, y_hbm,
                    hbuf, hsel, acc, gsem, ssem, *, ts):
    j = pl.program_id(0)
    f = pl.program_id(1)
    nj = pl.num_programs(0)
    nf = pl.num_programs(1)
    slot = j % 2

    def gather_copy(tile, sl_, r):
        return pltpu.make_async_copy(h_hbm.at[pl.ds(src_ref[tile * ts + r], 1)], hbuf.at[sl_, pl.ds(r, 1)],
                                     gsem.at[sl_])

    def scatter_copy(tile, r):
        return pltpu.make_async_copy(acc.at[pl.ds(r, 1)], y_hbm.at[pl.ds(dst_ref[tile * ts + r], 1)], ssem.at[0])

    def for_rows(fn):
        def body(r, carry):
            fn(r)
            return carry
        lax.fori_loop(0, ts, body, 0, unroll=8)

    @pl.when(f == 0)
    def _():
        @pl.when(j == 0)
        def _():
            for_rows(lambda r: gather_copy(j, slot, r).start())

        for_rows(lambda r: gather_copy(j, slot, r).wait())
        hsel[...] = hbuf[slot].astype(BF16)

        @pl.when(j > 0)
        def _():
            for_rows(lambda r: scatter_copy(j - 1, r).wait())

        acc[...] = jnp.zeros_like(acc)

    @pl.when((f == nf - 1) & (j + 1 < nj))
    def _():
        for_rows(lambda r: gather_copy(j + 1, 1 - slot, r).start())

    @pl.when(j < nact_ref[0])
    def _():
        h = hsel[...]
        fc = w2_ref.shape[0]
        a = jnp.dot(h, w13_ref[:, :fc], preferred_element_type=F32)
        bb = jnp.dot(h, w13_ref[:, fc:], preferred_element_type=F32)
        acc[...] += _bdot(_silu(a) * bb * wrow_ref[...], w2_ref[...])

    @pl.when(f == nf - 1)
    def _():
        for_rows(lambda r: scatter_copy(j, r).start())

        @pl.when(j == nj - 1)
        def _():
            for_rows(lambda r: scatter_copy(j, r).wait())


def _experts(h2, te, nact, src, dst, wrow, w13, w2, ts, n_out_rows):
    n, d = h2.shape
    _, nf, fc, _ = w2.shape
    nt = src.shape[0] // ts
    return pl.pallas_call(
        functools.partial(_experts_kernel, ts=ts),
        grid_spec=pltpu.PrefetchScalarGridSpec(
            num_scalar_prefetch=4,
            grid=(nt, nf),
            in_specs=[pl.BlockSpec(memory_space=pl.ANY),
                      pl.BlockSpec((ts, 1), lambda j, f, te_, na_, s_, d_: (j, 0)),
                      pl.BlockSpec((None, None, d, 2 * fc), lambda j, f, te_, na_, s_, d_: (te_[j], f, 0, 0)),
                      pl.BlockSpec((None, None, fc, d), lambda j, f, te_, na_, s_, d_: (te_[j], f, 0, 0))],
            out_specs=pl.BlockSpec(memory_space=pl.ANY),
            scratch_shapes=[pltpu.VMEM((2, ts, d), F32), pltpu.VMEM((ts, d), BF16), pltpu.VMEM((ts, d), F32),
                            pltpu.SemaphoreType.DMA((2,)), pltpu.SemaphoreType.DMA((1,))]),
        out_shape=jax.ShapeDtypeStruct((n_out_rows, d), F32),
        compiler_params=_params(("arbitrary",) * 2, 52 * 1024 * 1024, disable_bounds_checks=True),
        name="moe_experts",
    )(te, nact, src, dst, h2, wrow, w13, w2)


def _combine_kernel(x_ref, y0_ref, y1_ref, modc_ref, modb_ref, o_ref, *, tt, ctx_len, skip):
    i = pl.program_id(1) + skip
    o_ref[...] = x_ref[...] + _gate_rows(modc_ref, modb_ref, i * tt, tt, ctx_len, 5) * (y0_ref[...] + y1_ref[...])


def _combine(xa, y, modc, modb, tt, ctx_len, latent_only):
    b, ta, d = xa.shape
    nt = ta // tt
    skip = ctx_len // tt if latent_only else 0
    return pl.pallas_call(
        functools.partial(_combine_kernel, tt=tt, ctx_len=ctx_len, skip=skip),
        grid=(b, nt - skip),
        in_specs=[pl.BlockSpec((None, tt, d), lambda bb, i: (bb, i + skip, 0)),
                  pl.BlockSpec((tt, d), lambda bb, i: (bb * nt + i + skip, 0)),
                  pl.BlockSpec((tt, d), lambda bb, i: (b * nt + bb * nt + i + skip, 0)),
                  pl.BlockSpec((6, d), lambda bb, i: (0, 0)),
                  pl.BlockSpec((None, 6, d), lambda bb, i: (bb, 0, 0))],
        out_specs=pl.BlockSpec((None, tt, d), lambda bb, i: (bb, i, 0)),
        out_shape=jax.ShapeDtypeStruct((b, ta - skip * tt, d), F32),
        compiler_params=_params(("arbitrary",) * 2),
        name="moe_combine",
    )(xa, y, y, modc, modb)


def _moe(xa, g, modc, modb, router, w13, w2, tm, ctx_len, latent_only):
    b, ta, d = xa.shape
    n_exp = router.shape[-1]
    n = b * ta
    ts = MOE_TILE
    rt = jnp.zeros((d, 128), F32).at[:, :n_exp].set(router)
    h, route = _route(xa, g, modc, modb, rt, tm, ctx_len, n_exp)
    route = route.reshape(n, 128)
    e_s = route[:, :TOP_K].astype(jnp.int32).reshape(-1)
    w_s = route[:, TOP_K:2 * TOP_K].reshape(-1)
    onehot = (e_s[:, None] == jnp.arange(n_exp)[None, :]).astype(jnp.int32)
    cum = jnp.cumsum(onehot, axis=0)
    rank = jnp.take_along_axis(cum, e_s[:, None], axis=1)[:, 0] - 1
    padded = ((cum[-1] + ts - 1) // ts) * ts
    ends = jnp.cumsum(padded)
    pos = (ends - padded)[e_s] + rank
    p_rows = TOP_K * n + n_exp * ts
    inv = jnp.full((p_rows,), -1, jnp.int32).at[pos].set(jnp.arange(TOP_K * n, dtype=jnp.int32))
    real = inv >= 0
    row = jnp.maximum(inv, 0)
    src = row // TOP_K
    spare = TOP_K * n - 1 + jnp.cumsum(jnp.where(real, 0, 1))
    dst = jnp.where(real, (row % TOP_K) * n + row // TOP_K, spare).astype(jnp.int32)
    wrow = jnp.where(real, w_s[row], 0.0).reshape(p_rows, 1)
    tile_start = jnp.arange(p_rows // ts, dtype=jnp.int32) * ts
    te = jnp.minimum(jnp.sum(tile_start[:, None] >= ends[None, :], axis=1), n_exp - 1).astype(jnp.int32)
    nact = (ends[-1:] // ts).astype(jnp.int32)
    y = _experts(h.reshape(n, d), te, nact, src, dst, wrow, w13, w2, ts, p_rows)
    return _combine(xa, y, modc, modb, TOK_TILE, ctx_len, latent_only)


def _layer_weights(li, w_in, pool_w, pool_scale, shift_mu, decay_w0, decay_w2, iclr_a0, iclr_a2, key_kk, key_ka,
                   bonus_rk, gn_g, gn_b, gate_g2, nat_qn_g, nat_kn_g, nat_rpb, w_branch, w_out):
    c = BRANCH_W
    d = w_in.shape[1]
    wi = w_in[li]
    o_r = c
    o_lora = o_r + 3 * c
    n_lora = 2 * DECAY_LORA + 2 * ICLR_LORA + GATE_LORA
    o_q = o_lora + n_lora
    o_g = o_q + 3 * c
    pad = jnp.zeros((d, LORA_W - n_lora), F32)
    w_z = jnp.concatenate([wi[:, o_g:], wi[:, o_r:o_lora], wi[:, o_q:o_g], wi[:, :c], wi[:, o_lora:o_q], pad],
                          axis=1).astype(BF16)
    mu = shift_mu[li]
    mur = mu[:, :3 * c]
    mul = jnp.concatenate([mu[:, 3 * c:], jnp.zeros((2, LORA_W - n_lora), F32)], axis=1)
    w2f = jnp.zeros((LORA_W, 2 * c), F32)
    a2f = jnp.zeros((LORA_W, 2 * c), F32)
    for dd in range(2):
        w2f = w2f.at[dd * DECAY_LORA:(dd + 1) * DECAY_LORA, dd * c:(dd + 1) * c].set(decay_w2[li, dd])
        a2f = a2f.at[2 * DECAY_LORA + dd * ICLR_LORA:2 * DECAY_LORA + (dd + 1) * ICLR_LORA,
                     dd * c:(dd + 1) * c].set(iclr_a2[li, dd])
    o_gl = 2 * DECAY_LORA + 2 * ICLR_LORA
    g2f = jnp.zeros((LORA_W, c), F32).at[o_gl:o_gl + GATE_LORA].set(gate_g2[li])
    return dict(
        w_z=w_z, mur=mur, mul=mul,
        w0=decay_w0[li].reshape(1, 2 * c), w2f=w2f.astype(BF16),
        a0=iclr_a0[li].reshape(1, 2 * c), a2f=a2f.astype(BF16), g2f=g2f.astype(BF16),
        kkw=key_kk[li].reshape(1, c), kaw=key_ka[li].reshape(1, c), brk=bonus_rk[li].reshape(1, c),
        gng=gn_g[li].reshape(1, c), gnb=gn_b[li].reshape(1, c),
        gq=jnp.tile(nat_qn_g[li], HEADS).reshape(1, c), gk=jnp.tile(nat_kn_g[li], HEADS).reshape(1, c),
        tab=_nat_bias_table(nat_rpb[li], TOK_TILE),
        pool_w=pool_w[li].astype(BF16), pool_scale=pool_scale[li].reshape(1, c),
        wb=w_branch[li].astype(BF16), wo=w_out[li].astype(BF16),
    )


def kernel(x, c, ctx, c_ctx, w_mod, b_mod, norm_mix_g, norm_ffn_g, w_in, pool_w, pool_scale, shift_mu, decay_w0, decay_w2, iclr_a0, iclr_a2, key_kk, key_ka, bonus_rk, gn_g, gn_b, gate_g2, nat_qn_g, nat_kn_g, nat_rpb, w_branch, w_out, ffn_w1, ffn_w3, ffn_w2, router, moe_w1, moe_w3, moe_w2):
    b, t, d = x.shape
    ctx_len = ctx.shape[1]
    depth = w_mod.shape[0]
    ta = ctx_len + t
    tt = TOK_TILE
    assert ctx_len == tt and t % tt == 0 and t // GRID_W >= NAT_KH and b + 1 <= 8 and CHUNK == HD
    tm = ta // 8
    tm_ffn = ta // 16
    assert ta % 16 == 0 and tm_ffn % 16 == 0 and ffn_w1.shape[-1] % 256 == 0

    xa = jnp.concatenate([ctx, x], axis=1)
    cc = jnp.zeros((8, d), F32).at[:b].set(c).at[b].set(c_ctx)
    mod_all = _modulation(cc, w_mod, b_mod)
    head_id = jnp.arange(BRANCH_W) // HD
    bd = (head_id[:, None] == head_id[None, :]).astype(BF16)

    for li in range(depth):
        lw_ = _layer_weights(li, w_in, pool_w, pool_scale, shift_mu, decay_w0, decay_w2, iclr_a0, iclr_a2, key_kk,
                             key_ka, bonus_rk, gn_g, gn_b, gate_g2, nat_qn_g, nat_kn_g, nat_rpb, w_branch, w_out)
        modb = mod_all[li, :b].reshape(b, 6, d)
        modc = mod_all[li, b].reshape(6, d)
        z = _inproj(xa, norm_mix_g[li].reshape(1, d), modc, modb, lw_["w_z"], tt, ctx_len)
        pool = _pool(z, lw_["pool_w"], lw_["pool_scale"], tt, ctx_len)
        r, v, kk, lwd, kd, ka, bonus, gate = _prep(z, lw_["mur"], lw_["mul"], lw_["w0"], lw_["w2f"], lw_["a0"],
                                                   lw_["a2f"], lw_["g2f"], lw_["kkw"], lw_["kaw"], lw_["brk"], bd,
                                                   tt, ctx_len)
        o2 = _wkv(r, v, kk, lwd, kd, ka, ctx_len)
        qn, kn, vb = _qknorm(z, lw_["gq"], lw_["gk"], bd, tt)
        nat = _nat(qn, kn, vb, lw_["tab"], tt, ctx_len)
        xa = _merge(xa, pool, o2, bonus, gate, nat, z, modc, modb, lw_["gng"], lw_["gnb"], bd, lw_["wb"], lw_["wo"],
                    tt, ctx_len)
        j = li // 2
        gf = norm_ffn_g[li].reshape(1, d)
        if li % 2 == 0:
            xa = _ffn(xa, gf, modc, modb, *_swiglu_halves(ffn_w1[j], ffn_w3[j], ffn_w2[j]), tm_ffn, ctx_len)
        else:
            xa = _moe(xa, gf, modc, modb, router[j], *_swiglu_halves(moe_w1[j], moe_w3[j], moe_w2[j]), tm, ctx_len,
                      latent_only=li == depth - 1)
    return xa if xa.shape[1] == t else xa[:, ctx_len:]
```
